```python
import jax
import jax.numpy as jnp
from jax import lax
import numpy as np

D_MODEL = 2048
BATCH = 1
SEQ = 8192
DEPTH = 4

GRID_W = 64
CTX_LEN = 256
NORM_EPS = 1e-6

A_HEADS = 8
A_KDIM = 128
A_VDIM = 128
A_KEY = A_HEADS * A_KDIM
A_WIDTH = A_HEADS * A_VDIM
A_CHUNK = 64

B_WIDTH = 1024
B_KSIZE = 31

C_QHEADS = 8
C_KVHEADS = 2
C_HDIM = 128
C_WIDTH = C_QHEADS * C_HDIM
C_KVWIDTH = C_KVHEADS * C_HDIM
C_WINDOW = 128
C_BLOCK = 128
ROPE_BASE = 10000.0

N_BRANCH = 3

N_EXPERTS = 16
EC_CAPACITY = 2
EXPERT_DFF = 1024

IN_SIZES = (A_KEY, A_WIDTH, A_KEY, A_KEY, A_WIDTH, 2 * B_WIDTH, C_WIDTH, C_KVWIDTH, C_KVWIDTH, N_BRANCH * D_MODEL)

kernel_name = 'hybrid_hgrn2_conv_swa_ecmoe_dit'


def rms_norm(x, g):
    xf = x.astype(jnp.float32)
    y = xf * lax.rsqrt(jnp.mean(xf * xf, axis=-1, keepdims=True) + NORM_EPS)
    return (y * g.astype(jnp.float32)).astype(x.dtype)


def layer_norm(x, g, b):
    xf = x.astype(jnp.float32)
    xc = xf - jnp.mean(xf, axis=-1, keepdims=True)
    y = xc * lax.rsqrt(jnp.mean(xc * xc, axis=-1, keepdims=True) + NORM_EPS)
    return (y * g.astype(jnp.float32) + b.astype(jnp.float32)).astype(x.dtype)


def split_in(p):
    offsets = np.cumsum(IN_SIZES)[:-1].tolist()
    return jnp.split(p, offsets, axis=-1)


def flip_seq(t):
    return jnp.flip(t, axis=1)


def hgrn_lower_bounds(lb_logits):
    p = jax.nn.softmax(lb_logits.astype(jnp.float32), axis=0)
    cs = jnp.cumsum(p, axis=0)
    return cs - cs[0]


def hgrn_forget(z, lb):
    zf = z.astype(jnp.float32)
    log_f = jnp.logaddexp(jnp.log(lb), jnp.log1p(-lb) + jax.nn.log_sigmoid(zf))
    k = (1.0 - lb) * jax.nn.sigmoid(-zf)
    return log_f, k


def hgrn_chunk_scan(q, k, v, log_f, s0):
    bsz, seq_len, heads, _ = q.shape
    vdim = v.shape[-1]
    n_chunks = seq_len // A_CHUNK

    def chunks(t):
        return t.astype(jnp.float32).reshape(bsz, n_chunks, A_CHUNK, heads, t.shape[-1]).transpose(1, 0, 3, 2, 4)

    tri = jnp.tril(jnp.ones((A_CHUNK, A_CHUNK), dtype=bool))

    def step(s, inp):
        qc, kc, vc, gc = inp
        b = jnp.cumsum(gc, axis=2)
        rel = jnp.where(tri[:, :, None], b[:, :, :, None, :] - b[:, :, None, :, :], -jnp.inf)
        attn = jnp.einsum('bhtk,bhtsk,bhsk->bhts', qc, jnp.exp(rel), kc)
        o = jnp.einsum('bhts,bhsv->bhtv', attn, vc) + jnp.einsum('bhtk,bhkv->bhtv', qc * jnp.exp(b), s)
        b_end = b[:, :, -1, :]
        s_new = jnp.exp(b_end)[..., None] * s + jnp.einsum('bhsk,bhsv->bhkv', kc * jnp.exp(b_end[:, :, None, :] - b), vc)
        return s_new, o

    s_fin, o = lax.scan(step, s0, (chunks(q), chunks(k), chunks(v), chunks(log_f)))
    o = o.transpose(1, 0, 3, 2, 4).reshape(bsz, seq_len, heads, vdim)
    return o, s_fin


def hgrn_mixer(p_lat, p_ctx, lb_fwd, lb_bwd, onorm_g, with_ctx):
    def heads(t):
        return t.reshape(t.shape[0], t.shape[1], A_HEADS, -1)

    def prep(p):
        q, v, zf, zb, g = p
        lf_f, k_f = hgrn_forget(zf, lb_fwd)
        lf_b, k_b = hgrn_forget(zb, lb_bwd)
        return heads(q), heads(v), heads(lf_f), heads(k_f), heads(lf_b), heads(k_b), g

    def bidir(t, s0_f, s0_b):
        q, v, lf_f, k_f, lf_b, k_b, _ = t
        o_f, s_f = hgrn_chunk_scan(q, k_f, v, lf_f, s0_f)
        o_b, s_b = hgrn_chunk_scan(flip_seq(q), flip_seq(k_b), flip_seq(v), flip_seq(lf_b), s0_b)
        return o_f + flip_seq(o_b), s_f, s_b

    def readout(o, g):
        on = rms_norm(o, onorm_g.reshape(A_HEADS, A_VDIM))
        on = on.reshape(o.shape[0], o.shape[1], A_WIDTH)
        return (on * jax.nn.silu(g.astype(jnp.float32))).astype(g.dtype)

    t_ctx = prep(p_ctx)
    t_lat = prep(p_lat)
    s0 = jnp.zeros((p_lat[0].shape[0], A_HEADS, A_KDIM, A_VDIM), jnp.float32)
    o_ctx, s_f, s_b = bidir(t_ctx, s0, s0)
    o_lat, _, _ = bidir(t_lat, s_f, s_b)
    y_lat = readout(o_lat, t_lat[-1])
    y_ctx = readout(o_ctx, t_ctx[-1]) if with_ctx else None
    return y_lat, y_ctx


def conv_module(u, conv_w, conv_b, ln_g, ln_b):
    a, gate = jnp.split(u, 2, axis=-1)
    h = a * jax.nn.sigmoid(gate)
    h = lax.conv_general_dilated(h, conv_w[:, None, :], window_strides=(1,),
                                 padding=[(B_KSIZE // 2, B_KSIZE // 2)],
                                 dimension_numbers=('NWC', 'WIO', 'NWC'),
                                 feature_group_count=B_WIDTH) + conv_b
    return jax.nn.silu(layer_norm(h, ln_g, ln_b))


def axial_rope_tables(s_len):
    n_rows = s_len // GRID_W
    rows = jnp.broadcast_to(jnp.arange(n_rows, dtype=jnp.float32)[:, None], (n_rows, GRID_W)).reshape(-1)
    cols = jnp.broadcast_to(jnp.arange(GRID_W, dtype=jnp.float32)[None, :], (n_rows, GRID_W)).reshape(-1)
    half = C_HDIM // 2
    inv = ROPE_BASE ** (-jnp.arange(0, half, 2, dtype=jnp.float32) / half)
    ang_r = rows[:, None] * inv
    ang_c = cols[:, None] * inv
    return jnp.cos(ang_r), jnp.sin(ang_r), jnp.cos(ang_c), jnp.sin(ang_c)


def apply_axial_rope(x, tables):
    cos_r, sin_r, cos_c, sin_c = tables
    half = C_HDIM // 2

    def rot(xh, cos, sin):
        x1, x2 = jnp.split(xh.astype(jnp.float32), 2, axis=-1)
        cos = cos[None, :, None, :]
        sin = sin[None, :, None, :]
        return jnp.concatenate([x1 * cos - x2 * sin, x1 * sin + x2 * cos], axis=-1)

    out = jnp.concatenate([rot(x[..., :half], cos_r, sin_r), rot(x[..., half:], cos_c, sin_c)], axis=-1)
    return out.astype(x.dtype)


def softmax_with_sink(logits, sink):
    sink_col = jnp.broadcast_to(sink, logits.shape[:-1] + (1,))
    p = jax.nn.softmax(jnp.concatenate([logits, sink_col], axis=-1), axis=-1)
    return p[..., :-1]


def window_attention(p_lat, p_ctx, rope, sink, with_ctx):
    q, k, v = p_lat
    cq, ck, cv = p_ctx
    bsz, s_len = q.shape[0], q.shape[1]
    n_ctx = ck.shape[1]
    grp = C_QHEADS // C_KVHEADS
    nb = s_len // C_BLOCK
    scale = C_HDIM ** -0.5
    sink_g = sink.astype(jnp.float32).reshape(C_KVHEADS, grp)

    q = apply_axial_rope(q.reshape(bsz, s_len, C_QHEADS, C_HDIM), rope)
    k = apply_axial_rope(k.reshape(bsz, s_len, C_KVHEADS, C_HDIM), rope)
    v = v.reshape(bsz, s_len, C_KVHEADS, C_HDIM)
    ck = ck.reshape(bsz, n_ctx, C_KVHEADS, C_HDIM)
    cv = cv.reshape(bsz, n_ctx, C_KVHEADS, C_HDIM)

    qb = q.reshape(bsz, nb, C_BLOCK, C_KVHEADS, grp, C_HDIM)
    pad = ((0, 0), (C_BLOCK, C_BLOCK), (0, 0), (0, 0))
    kp = jnp.pad(k, pad).reshape(bsz, nb + 2, C_BLOCK, C_KVHEADS, C_HDIM)
    vp = jnp.pad(v, pad).reshape(bsz, nb + 2, C_BLOCK, C_KVHEADS, C_HDIM)
    kn = jnp.concatenate([kp[:, 0:nb], kp[:, 1:nb + 1], kp[:, 2:nb + 2]], axis=2)
    vn = jnp.concatenate([vp[:, 0:nb], vp[:, 1:nb + 1], vp[:, 2:nb + 2]], axis=2)

    qpos = jnp.arange(nb)[:, None] * C_BLOCK + jnp.arange(C_BLOCK)[None, :]
    kpos = (jnp.arange(nb)[:, None] - 1) * C_BLOCK + jnp.arange(3 * C_BLOCK)[None, :]
    valid = ((jnp.abs(qpos[:, :, None] - kpos[:, None, :]) <= C_WINDOW)
             & (kpos[:, None, :] >= 0) & (kpos[:, None, :] < s_len))

    s_loc = jnp.einsum('bnqhgd,bnkhd->bnhgqk', qb, kn).astype(jnp.float32) * scale
    s_loc = jnp.where(valid[None, :, None, None], s_loc, -jnp.inf)
    s_ctx = jnp.einsum('bnqhgd,bkhd->bnhgqk', qb, ck).astype(jnp.float32) * scale
    p = softmax_with_sink(jnp.concatenate([s_loc, s_ctx], axis=-1), sink_g[None, None, :, :, None, None])
    p_loc = p[..., :3 * C_BLOCK].astype(v.dtype)
    p_ctx = p[..., 3 * C_BLOCK:].astype(v.dtype)
    o = jnp.einsum('bnhgqk,bnkhd->bnqhgd', p_loc, vn) + jnp.einsum('bnhgqk,bkhd->bnqhgd', p_ctx, cv)
    y_lat = o.reshape(bsz, s_len, C_WIDTH)

    y_ctx = None
    if with_ctx:
        cqg = cq.reshape(bsz, n_ctx, C_KVHEADS, grp, C_HDIM)
        sc = jnp.einsum('bqhgd,bkhd->bhgqk', cqg, ck).astype(jnp.float32) * scale
        pc = softmax_with_sink(sc, sink_g[None, :, :, None, None]).astype(cv.dtype)
        y_ctx = jnp.einsum('bhgqk,bkhd->bqhgd', pc, cv).reshape(bsz, n_ctx, C_WIDTH)
    return y_lat, y_ctx


def token_mixer(hx, hc, rope, lb_fwd, lb_bwd, w_in, onorm_g, conv_w, conv_b, ln_g, ln_b, sink,
                w_a, w_b, w_c, w_out, with_ctx):
    px = split_in(hx @ w_in)
    pc = split_in(hc @ w_in)
    ya_x, ya_c = hgrn_mixer(px[0:5], pc[0:5], lb_fwd, lb_bwd, onorm_g, with_ctx)
    yc_x, yc_c = window_attention(px[6:9], pc[6:9], rope, sink, with_ctx)
    yb_x = conv_module(px[5], conv_w, conv_b, ln_g, ln_b)

    def merge(ya, yb, yc, gate_logits):
        ga, gb, gc = jnp.split(jax.nn.sigmoid(gate_logits), N_BRANCH, axis=-1)
        return (ga * (ya @ w_a) + gb * (yb @ w_b) + gc * (yc @ w_c)) @ w_out

    y_x = merge(ya_x, yb_x, yc_x, px[9])
    y_c = None
    if with_ctx:
        yb_c = conv_module(pc[5], conv_w, conv_b, ln_g, ln_b)
        y_c = merge(ya_c, yb_c, yc_c, pc[9])
    return y_x, y_c


def expert_choice_moe(h, w_router, w_gate, w_up, w_down):
    bsz, n_tok, dim = h.shape
    cap = EC_CAPACITY * n_tok // N_EXPERTS
    aff = jax.nn.softmax((h @ w_router).astype(jnp.float32), axis=-1)
    gate, idx = lax.top_k(jnp.swapaxes(aff, 1, 2), cap)
    xs = jax.vmap(lambda hb, ib: hb[ib])(h, idx)
    hid = jax.nn.silu(jnp.einsum('becd,edf->becf', xs, w_gate)) * jnp.einsum('becd,edf->becf', xs, w_up)
    ys = jnp.einsum('becf,efd->becd', hid, w_down) * gate[..., None].astype(h.dtype)
    return jax.vmap(lambda ib, yb: jnp.zeros((n_tok, dim), yb.dtype).at[ib.reshape(-1)].add(yb.reshape(-1, dim)))(idx, ys)


def setup_inputs(seed: int = 0) -> dict:
    key = jax.random.key(seed)
    ks = jax.random.split(key, 32)
    f32 = jnp.float32
    dm = D_MODEL
    d_in = sum(IN_SIZES)

    def nrm(k, shape, scale):
        return jax.random.normal(k, shape, f32) * scale

    return {
        'x': nrm(ks[0], (BATCH, SEQ, dm), 1.0),
        'c': nrm(ks[1], (BATCH, dm), 1.0),
        'ctx': nrm(ks[2], (BATCH, CTX_LEN, dm), 1.0),
        'c_ctx': nrm(ks[3], (dm,), 1.0),
        'w_ada': nrm(ks[4], (DEPTH, dm, 6 * dm), 0.5 * dm ** -0.5),
        'b_ada': nrm(ks[5], (DEPTH, 6 * dm), 0.02),
        'norm1_g': 1.0 + nrm(ks[6], (DEPTH, dm), 0.02),
        'norm2_g': 1.0 + nrm(ks[7], (DEPTH, dm), 0.02),
        'w_in': nrm(ks[8], (DEPTH, dm, d_in), dm ** -0.5),
        'hgrn_lb_logits': nrm(ks[9], (DEPTH, 2, A_KEY), 0.5),
        'hgrn_onorm_g': 1.0 + nrm(ks[10], (DEPTH, A_WIDTH), 0.02),
        'conv_w': nrm(ks[11], (DEPTH, B_KSIZE, B_WIDTH), B_KSIZE ** -0.5),
        'conv_b': nrm(ks[12], (DEPTH, B_WIDTH), 0.02),
        'conv_ln_g': 1.0 + nrm(ks[13], (DEPTH, B_WIDTH), 0.02),
        'conv_ln_b': nrm(ks[14], (DEPTH, B_WIDTH), 0.02),
        'attn_sink': nrm(ks[15], (DEPTH, C_QHEADS), 0.5),
        'w_branch_a': nrm(ks[16], (DEPTH, A_WIDTH, dm), A_WIDTH ** -0.5),
        'w_branch_b': nrm(ks[17], (DEPTH, B_WIDTH, dm), B_WIDTH ** -0.5),
        'w_branch_c': nrm(ks[18], (DEPTH, C_WIDTH, dm), C_WIDTH ** -0.5),
        'w_out': nrm(ks[19], (DEPTH, dm, dm), dm ** -0.5),
        'w_router': nrm(ks[20], (DEPTH, dm, N_EXPERTS), dm ** -0.5),
        'w_exp_gate': nrm(ks[21], (DEPTH, N_EXPERTS, dm, EXPERT_DFF), dm ** -0.5),
        'w_exp_up': nrm(ks[22], (DEPTH, N_EXPERTS, dm, EXPERT_DFF), dm ** -0.5),
        'w_exp_down': nrm(ks[23], (DEPTH, N_EXPERTS, EXPERT_DFF, dm), EXPERT_DFF ** -0.5),
        'final_norm_g': 1.0 + nrm(ks[24], (dm,), 0.02),
    }


def reference(x, c, ctx, c_ctx, w_ada, b_ada, norm1_g, norm2_g, w_in, hgrn_lb_logits, hgrn_onorm_g,
              conv_w, conv_b, conv_ln_g, conv_ln_b, attn_sink, w_branch_a, w_branch_b, w_branch_c,
              w_out, w_router, w_exp_gate, w_exp_up, w_exp_down, final_norm_g):
    rope = axial_rope_tables(x.shape[1])
    lbs = hgrn_lower_bounds(hgrn_lb_logits)
    c_lat = jax.nn.silu(c)
    c_con = jax.nn.silu(c_ctx)
    cx = ctx
    for l in range(DEPTH):
        with_ctx = l < DEPTH - 1
        mod_x = jnp.split((c_lat @ w_ada[l] + b_ada[l])[:, None, :], 6, axis=-1)
        mod_c = jnp.split(c_con @ w_ada[l] + b_ada[l], 6, axis=-1)
        hx = rms_norm(x, norm1_g[l]) * (1.0 + mod_x[1]) + mod_x[0]
        hc = rms_norm(cx, norm1_g[l]) * (1.0 + mod_c[1]) + mod_c[0]
        y_x, y_c = token_mixer(hx, hc, rope, lbs[l, 0], lbs[l, 1], w_in[l], hgrn_onorm_g[l],
                               conv_w[l], conv_b[l], conv_ln_g[l], conv_ln_b[l], attn_sink[l],
                               w_branch_a[l], w_branch_b[l], w_branch_c[l], w_out[l], with_ctx)
        x = x + mod_x[2] * y_x
        hx = rms_norm(x, norm2_g[l]) * (1.0 + mod_x[4]) + mod_x[3]
        x = x + mod_x[5] * expert_choice_moe(hx, w_router[l], w_exp_gate[l], w_exp_up[l], w_exp_down[l])
        if with_ctx:
            cx = cx + mod_c[2] * y_c
            hc = rms_norm(cx, norm2_g[l]) * (1.0 + mod_c[4]) + mod_c[3]
            cx = cx + mod_c[5] * expert_choice_moe(hc, w_router[l], w_exp_gate[l], w_exp_up[l], w_exp_down[l])
    return rms_norm(x, final_norm_g)
```

```python
import functools

import jax
import jax.numpy as jnp
import numpy as np
from jax import lax
from jax.experimental import pallas as pl
from jax.experimental.pallas import tpu as pltpu

F32 = jnp.float32
BF = jnp.bfloat16
HI = lax.Precision.HIGHEST

D = 2048
EPS = 1e-6
GRID_W = 64
ROPE_BASE = 10000.0

A_HEADS = 8
A_DIM = 128
A_WIDTH = A_HEADS * A_DIM
A_CHUNK = 64
A_SAFE_RANGE = 120.0

B_WIDTH = 1024
B_KSIZE = 31
B_HALO = 16

C_QHEADS = 8
C_KVHEADS = 2
C_GROUP = C_QHEADS // C_KVHEADS
C_HDIM = 128
C_BLOCK = 128

N_EXPERTS = 16
EC_CAPACITY = 2
EXPERT_DFF = 1024
FFN_FCHUNK = 256

ROW_BLOCK = 256
LANES = 128
VMEM_LIMIT = 56 * 1024 * 1024

SEG_QV = (0, 2 * A_WIDTH)
SEG_Z = (2 * A_WIDTH, 4 * A_WIDTH)
SEG_REST0 = 4 * A_WIDTH
REST_G = 0
REST_CONV_A = A_WIDTH
REST_CONV_G = A_WIDTH + B_WIDTH
REST_CQ = A_WIDTH + 2 * B_WIDTH
REST_CK = REST_CQ + C_QHEADS * C_HDIM
REST_CV = REST_CK + C_KVHEADS * C_HDIM
REST_GATES = REST_CV + C_KVHEADS * C_HDIM
REST_WIDTH = REST_GATES + 3 * D
D_IN = SEG_REST0 + REST_WIDTH


def _cparams(sem, vmem=VMEM_LIMIT):
    return pltpu.CompilerParams(dimension_semantics=sem, vmem_limit_bytes=vmem)


def _silu(x):
    return x * jax.nn.sigmoid(x)


def _mod_kernel(c_ref, w_ref, b_ref, o_ref):
    s = _silu(c_ref[...]).astype(BF)
    o_ref[0] = jnp.dot(s, w_ref[0], preferred_element_type=F32) + b_ref[0]


def _modulation(c8, w_ada, b_ada):
    depth, _, n6 = w_ada.shape
    tn = D
    return pl.pallas_call(
        _mod_kernel,
        out_shape=jax.ShapeDtypeStruct((depth, 8, n6), F32),
        grid=(depth, n6 // tn),
        in_specs=[pl.BlockSpec((8, D), lambda l, n: (0, 0)),
                  pl.BlockSpec((1, D, tn), lambda l, n: (l, 0, n)),
                  pl.BlockSpec((1, 1, tn), lambda l, n: (l, 0, n))],
        out_specs=pl.BlockSpec((1, 8, tn), lambda l, n: (l, 0, n)),
        compiler_params=_cparams(("arbitrary", "arbitrary")),
        name="adaln_mod",
    )(c8, w_ada, b_ada.reshape(depth, 1, n6))


def _mod_row(mod_ref, is_ctx, i):
    row = jnp.where(is_ctx, 1, 0)
    return mod_ref[pl.ds(row, 1), i * D:(i + 1) * D]


def _norm_mod_kernel(x_ref, g_ref, mod_ref, o_ref, *, shift_i, scale_i, n_ctx_blocks):
    x = x_ref[...]
    y = x * lax.rsqrt(jnp.mean(x * x, axis=-1, keepdims=True) + EPS) * g_ref[...]
    is_ctx = pl.program_id(0) < n_ctx_blocks
    y = y * (1.0 + _mod_row(mod_ref, is_ctx, scale_i)) + _mod_row(mod_ref, is_ctx, shift_i)
    o_ref[...] = y.astype(o_ref.dtype)


def _norm_mod(x, g, mod, shift_i, scale_i, n_ctx_blocks, out_dtype):
    r = x.shape[0]
    return pl.pallas_call(
        functools.partial(_norm_mod_kernel, shift_i=shift_i, scale_i=scale_i, n_ctx_blocks=n_ctx_blocks),
        out_shape=jax.ShapeDtypeStruct((r, D), out_dtype),
        grid=(r // ROW_BLOCK,),
        in_specs=[pl.BlockSpec((ROW_BLOCK, D), lambda i: (i, 0)),
                  pl.BlockSpec((1, D), lambda i: (0, 0)),
                  pl.BlockSpec((8, 6 * D), lambda i: (0, 0))],
        out_specs=pl.BlockSpec((ROW_BLOCK, D), lambda i: (i, 0)),
        compiler_params=_cparams(("arbitrary",)),
        name="norm_mod",
    )(x, g.reshape(1, D), mod)


def _mm_kernel(a_ref, b_ref, o_ref):
    o_ref[...] = jnp.dot(a_ref[...], b_ref[0], preferred_element_type=F32).astype(o_ref.dtype)


def _row_tile(r):
    return 768 if r % 768 == 0 else ROW_BLOCK


def _matmul_cols(a, w3, layer, col0, ncols, out_dtype, tn=512):
    m, k = a.shape
    tm = _row_tile(m)
    cb0 = col0 // tn
    return pl.pallas_call(
        _mm_kernel,
        out_shape=jax.ShapeDtypeStruct((m, ncols), out_dtype),
        grid=(ncols // tn, m // tm),
        in_specs=[pl.BlockSpec((tm, k), lambda n, i: (i, 0)),
                  pl.BlockSpec((1, k, tn), lambda n, i: (layer, 0, cb0 + n))],
        out_specs=pl.BlockSpec((tm, tn), lambda n, i: (i, n)),
        compiler_params=_cparams(("arbitrary", "arbitrary")),
        name="in_proj",
    )(a, w3)


def _hgrn_kernel(*refs, backward, chunks):
    if backward:
        (q_ref, v_ref, z_ref, lb_ref, of_ref, g_ref, og_ref, out_ref,
         st_ref, a_scr, b_scr, k_scr, o_scr) = refs
    else:
        q_ref, v_ref, z_ref, lb_ref, out_ref, st_ref, a_scr, b_scr, k_scr = refs
        o_scr = out_ref
    cs = A_CHUNK

    @pl.when(pl.program_id(0) == 0)
    def _():
        st_ref[...] = jnp.zeros_like(st_ref)

    lb = lb_ref[...]
    z = z_ref[...]
    log_lb = jnp.log(lb)
    log_sig = jnp.minimum(z, 0.0) - jnp.log(1.0 + jnp.exp(-jnp.abs(z)))
    c = jnp.log(1.0 - lb) + log_sig
    lf = jnp.maximum(log_lb, c) + jnp.log(1.0 + jnp.exp(-jnp.abs(log_lb - c)))
    kk = (1.0 - lb) / (1.0 + jnp.exp(z))

    lo = None
    for ci in range(chunks):
        t = jnp.sum(lf[ci * cs:(ci + 1) * cs], axis=0, keepdims=True)
        lo = t if lo is None else jnp.minimum(lo, t)
    safe = jnp.min(lo) >= -A_SAFE_RANGE

    ri = lax.broadcasted_iota(jnp.int32, (cs, cs), 0)
    cj = lax.broadcasted_iota(jnp.int32, (cs, cs), 1)
    causal = (cj >= ri) if backward else (cj <= ri)
    tri = jnp.where(causal, 1.0, 0.0).astype(F32)
    lane_cs = lax.broadcasted_iota(jnp.int32, (1, cs), 1)
    hi = lax.broadcasted_iota(jnp.int32, (A_WIDTH, LANES), 0) // A_DIM
    hj = lax.broadcasted_iota(jnp.int32, (A_WIDTH, LANES), 1)
    head_ind = jnp.where(hi == hj, 1.0, 0.0).astype(F32)

    order = range(chunks - 1, -1, -1) if backward else range(chunks)
    for ci in order:
        rows = slice(ci * cs, (ci + 1) * cs)
        q_c = q_ref[rows, :].astype(F32)
        v_c = v_ref[rows, :]
        k_c = kk[rows]
        b = jnp.dot(tri, lf[rows], precision=HI, preferred_element_type=F32)
        b_lo, b_hi = b[0:1], b[cs - 1:cs]
        tot = b_lo if backward else b_hi

        @pl.when(safe)
        def _():
            r = 0.5 * (b_lo + b_hi)
            qa = (q_c * jnp.exp(b - r)).astype(BF)
            kb = (k_c * jnp.exp(r - b)).astype(BF)
            for h in range(A_HEADS):
                hs = slice(h * A_DIM, (h + 1) * A_DIM)
                a_scr[h] = lax.dot_general(qa[:, hs], kb[:, hs], (((1,), (1,)), ((), ())),
                                           preferred_element_type=F32)

        @pl.when(jnp.logical_not(safe))
        def _():
            b_scr[...] = b
            k_scr[...] = k_c

            def body(s, acc):
                rb = b_scr[pl.ds(s, 1), :]
                rk = k_scr[pl.ds(s, 1), :]
                p = q_c * jnp.exp(jnp.minimum(b - rb, 0.0)) * rk
                col = jnp.dot(p, head_ind, precision=HI, preferred_element_type=F32)
                onehot = jnp.where(lane_cs == s, 1.0, 0.0)
                return tuple(acc[h] + col[:, h:h + 1] * onehot for h in range(A_HEADS))

            acc = lax.fori_loop(0, cs, body, tuple(jnp.zeros((cs, cs), F32) for _ in range(A_HEADS)))
            for h in range(A_HEADS):
                a_scr[h] = acc[h]

        qe = (q_c * jnp.exp(b)).astype(BF)
        ke = (k_c * jnp.exp(tot - b)).astype(BF)
        et = jnp.exp(tot)
        for h in range(A_HEADS):
            hs = slice(h * A_DIM, (h + 1) * A_DIM)
            am = jnp.where(causal, a_scr[h], 0.0).astype(BF)
            st = st_ref[h]
            o_h = jnp.dot(am, v_c[:, hs], preferred_element_type=F32)
            o_h = o_h + lax.dot_general(qe[:, hs], st.astype(BF), (((1,), (1,)), ((), ())),
                                        preferred_element_type=F32)
            st_ref[h] = st * et[:, hs] + lax.dot_general(v_c[:, hs], ke[:, hs], (((0,), (0,)), ((), ())),
                                                         preferred_element_type=F32)
            o_scr[rows, hs] = o_h

    if backward:
        o = of_ref[...] + o_scr[...]
        gate = _silu(g_ref[...].astype(F32))
        og = og_ref[...]
        for h in range(A_HEADS):
            hs = slice(h * A_DIM, (h + 1) * A_DIM)
            o_h = o[:, hs]
            y = o_h * lax.rsqrt(jnp.mean(o_h * o_h, axis=-1, keepdims=True) + EPS) * og[:, hs]
            out_ref[:, hs] = (y * gate[:, hs]).astype(out_ref.dtype)


def _hgrn_scratch(rows, backward):
    s = [pltpu.VMEM((A_HEADS, A_DIM, A_DIM), F32),
         pltpu.VMEM((A_HEADS, A_CHUNK, A_CHUNK), F32),
         pltpu.VMEM((A_CHUNK, A_WIDTH), F32),
         pltpu.VMEM((A_CHUNK, A_WIDTH), F32)]
    if backward:
        s.append(pltpu.VMEM((rows, A_WIDTH), F32))
    return s


def _hgrn_forward(p_qv, p_z, lb_fwd):
    r = p_qv.shape[0]
    rb = ROW_BLOCK
    return pl.pallas_call(
        functools.partial(_hgrn_kernel, backward=False, chunks=rb // A_CHUNK),
        out_shape=jax.ShapeDtypeStruct((r, A_WIDTH), F32),
        grid=(r // rb,),
        in_specs=[pl.BlockSpec((rb, A_WIDTH), lambda i: (i, 0)),
                  pl.BlockSpec((rb, A_WIDTH), lambda i: (i, 1)),
                  pl.BlockSpec((rb, A_WIDTH), lambda i: (i, 0)),
                  pl.BlockSpec((1, A_WIDTH), lambda i: (0, 0))],
        out_specs=pl.BlockSpec((rb, A_WIDTH), lambda i: (i, 0)),
        scratch_shapes=_hgrn_scratch(rb, False),
        compiler_params=_cparams(("arbitrary",)),
        name="hgrn_fwd",
    )(p_qv, p_qv, p_z, lb_fwd.reshape(1, A_WIDTH))


def _hgrn_backward(p_qv, p_z, p_rest, o_f, lb_bwd, onorm_g, n_ctx_blocks):
    r = p_qv.shape[0]
    rb = ROW_BLOCK
    nblk = r // rb

    def blk(s):
        return jnp.where(s < n_ctx_blocks, n_ctx_blocks - 1 - s, nblk - 1 - (s - n_ctx_blocks))

    g_col = REST_G // A_WIDTH
    return pl.pallas_call(
        functools.partial(_hgrn_kernel, backward=True, chunks=rb // A_CHUNK),
        out_shape=jax.ShapeDtypeStruct((r, A_WIDTH), BF),
        grid=(nblk,),
        in_specs=[pl.BlockSpec((rb, A_WIDTH), lambda s: (blk(s), 0)),
                  pl.BlockSpec((rb, A_WIDTH), lambda s: (blk(s), 1)),
                  pl.BlockSpec((rb, A_WIDTH), lambda s: (blk(s), 1)),
                  pl.BlockSpec((1, A_WIDTH), lambda s: (0, 0)),
                  pl.BlockSpec((rb, A_WIDTH), lambda s: (blk(s), 0)),
                  pl.BlockSpec((rb, A_WIDTH), lambda s: (blk(s), g_col)),
                  pl.BlockSpec((1, A_WIDTH), lambda s: (0, 0))],
        out_specs=pl.BlockSpec((rb, A_WIDTH), lambda s: (blk(s), 0)),
        scratch_shapes=_hgrn_scratch(rb, True),
        compiler_params=_cparams(("arbitrary",)),
        name="hgrn_bwd",
    )(p_qv, p_qv, p_z, lb_bwd.reshape(1, A_WIDTH), o_f, p_rest, onorm_g.reshape(1, A_WIDTH))


def _conv_kernel(ap_ref, ac_ref, an_ref, gp_ref, gc_ref, gn_ref, w_ref, b_ref, lg_ref, lbias_ref,
                 o_ref, hext, acc_scr, *, nblk, n_ctx_blocks):
    i = pl.program_id(0)
    rb = ROW_BLOCK

    def glu(a_ref, g_ref):
        return a_ref[...].astype(F32) * jax.nn.sigmoid(g_ref[...].astype(F32))

    prev_ok = jnp.logical_and(i != 0, i != n_ctx_blocks)
    next_ok = jnp.logical_and(i != n_ctx_blocks - 1, i != nblk - 1)
    hext[0:B_HALO] = jnp.where(prev_ok, glu(ap_ref, gp_ref), 0.0)
    hext[B_HALO:B_HALO + rb] = glu(ac_ref, gc_ref)
    hext[B_HALO + rb:B_HALO + rb + B_HALO] = jnp.where(next_ok, glu(an_ref, gn_ref), 0.0)

    row0 = B_HALO - B_KSIZE // 2
    rt, ct = 64, 256
    for cc in range(B_WIDTH // ct):
        cs = slice(cc * ct, (cc + 1) * ct)
        for rc in range(rb // rt):
            acc = jnp.zeros((rt, ct), F32)
            for j in range(B_KSIZE):
                acc = acc + w_ref[j:j + 1, cs] * hext[pl.ds(row0 + j + rc * rt, rt), cs]
            acc_scr[rc * rt:(rc + 1) * rt, cs] = acc + b_ref[:, cs]

    h = acc_scr[...]
    hc = h - jnp.mean(h, axis=-1, keepdims=True)
    y = hc * lax.rsqrt(jnp.mean(hc * hc, axis=-1, keepdims=True) + EPS) * lg_ref[...] + lbias_ref[...]
    o_ref[...] = _silu(y).astype(o_ref.dtype)


def _conv_module(p_rest, conv_w, conv_b, ln_g, ln_b, n_ctx_blocks):
    r = p_rest.shape[0]
    rb = ROW_BLOCK
    nblk = r // rb
    hpb = rb // B_HALO
    nh = r // B_HALO
    ca, cg = REST_CONV_A // B_WIDTH, REST_CONV_G // B_WIDTH

    def prev(i):
        return jnp.maximum(i * hpb - 1, 0)

    def nxt(i):
        return jnp.minimum((i + 1) * hpb, nh - 1)

    vec = lambda: pl.BlockSpec((1, B_WIDTH), lambda i: (0, 0))
    return pl.pallas_call(
        functools.partial(_conv_kernel, nblk=nblk, n_ctx_blocks=n_ctx_blocks),
        out_shape=jax.ShapeDtypeStruct((r, B_WIDTH), BF),
        grid=(nblk,),
        in_specs=[pl.BlockSpec((B_HALO, B_WIDTH), lambda i: (prev(i), ca)),
                  pl.BlockSpec((rb, B_WIDTH), lambda i: (i, ca)),
                  pl.BlockSpec((B_HALO, B_WIDTH), lambda i: (nxt(i), ca)),
                  pl.BlockSpec((B_HALO, B_WIDTH), lambda i: (prev(i), cg)),
                  pl.BlockSpec((rb, B_WIDTH), lambda i: (i, cg)),
                  pl.BlockSpec((B_HALO, B_WIDTH), lambda i: (nxt(i), cg)),
                  pl.BlockSpec((B_KSIZE, B_WIDTH), lambda i: (0, 0)),
                  vec(), vec(), vec()],
        out_specs=pl.BlockSpec((rb, B_WIDTH), lambda i: (i, 0)),
        scratch_shapes=[pltpu.VMEM((rb + 2 * B_HALO, B_WIDTH), F32), pltpu.VMEM((rb, B_WIDTH), F32)],
        compiler_params=_cparams(("arbitrary",)),
        name="conv_module",
    )(p_rest, p_rest, p_rest, p_rest, p_rest, p_rest, conv_w,
      conv_b.reshape(1, B_WIDTH), ln_g.reshape(1, B_WIDTH), ln_b.reshape(1, B_WIDTH))


def _rope(x, cos, sin):
    lane = lax.broadcasted_iota(jnp.int32, x.shape, 1)
    quarter = C_HDIM // 4
    swapped = jnp.where((lane & (2 * quarter - 1)) < quarter,
                        pltpu.roll(x, C_HDIM - quarter, 1), pltpu.roll(x, quarter, 1))
    return x * cos + swapped * sin


def _attn_kernel(sink_ref, q_ref, kp_ref, kc_ref, kn_ref, vp_ref, vc_ref, vn_ref, ck_ref, cv_ref,
                 cq_ref, sq_ref, cp_ref, sp_ref, cc_ref, sc_ref, cn_ref, sn_ref, o_ref,
                 *, n_ctx_blocks, seq, n_ctx):
    i = pl.program_id(0)
    blk = C_BLOCK
    nloc = 3 * blk
    nq = C_GROUP * blk
    scale = C_HDIM ** -0.5

    row = lax.broadcasted_iota(jnp.int32, (nq, nloc), 0)
    col = lax.broadcasted_iota(jnp.int32, (nq, nloc), 1)
    t = row & (blk - 1)
    kpos = (i - n_ctx_blocks - 1) * blk + col
    valid = (col >= t) & (col <= t + 2 * blk) & (kpos >= 0) & (kpos < seq) & (i >= n_ctx_blocks)
    rgrp = lax.broadcasted_iota(jnp.int32, (nq, 1), 0) // blk

    cq, sq = cq_ref[...], sq_ref[...]
    for g in range(C_KVHEADS):
        ks = slice(g * C_HDIM, (g + 1) * C_HDIM)
        kparts = [_rope(kp_ref[:, ks].astype(F32), cp_ref[...], sp_ref[...]).astype(BF),
                  _rope(kc_ref[:, ks].astype(F32), cc_ref[...], sc_ref[...]).astype(BF),
                  _rope(kn_ref[:, ks].astype(F32), cn_ref[...], sn_ref[...]).astype(BF),
                  ck_ref[:, ks]]
        k_all = jnp.concatenate(kparts, axis=0)
        v_all = jnp.concatenate([vp_ref[:, ks], vc_ref[:, ks], vn_ref[:, ks], cv_ref[:, ks]], axis=0)
        qparts, sink_col = [], jnp.zeros((nq, 1), F32)
        for j in range(C_GROUP):
            h = g * C_GROUP + j
            qparts.append(_rope(q_ref[:, h * C_HDIM:(h + 1) * C_HDIM].astype(F32), cq, sq).astype(BF))
            sink_col = jnp.where(rgrp == j, sink_ref[h], sink_col)
        q_all = jnp.concatenate(qparts, axis=0)
        s = lax.dot_general(q_all, k_all, (((1,), (1,)), ((), ())), preferred_element_type=F32) * scale
        s_loc = jnp.where(valid, s[:, :nloc], -1e30)
        s_ctx = s[:, nloc:]
        m = jnp.maximum(jnp.maximum(jnp.max(s_loc, axis=-1, keepdims=True),
                                    jnp.max(s_ctx, axis=-1, keepdims=True)), sink_col)
        p_loc = jnp.exp(s_loc - m)
        p_ctx = jnp.exp(s_ctx - m)
        den = (jnp.sum(p_loc, axis=-1, keepdims=True) + jnp.sum(p_ctx, axis=-1, keepdims=True)
               + jnp.exp(sink_col - m))
        o = jnp.dot(p_loc.astype(BF), v_all[:nloc], preferred_element_type=F32)
        o = o + jnp.dot(p_ctx.astype(BF), v_all[nloc:], preferred_element_type=F32)
        o = o / den
        for j in range(C_GROUP):
            h = g * C_GROUP + j
            o_ref[:, h * C_HDIM:(h + 1) * C_HDIM] = o[j * blk:(j + 1) * blk].astype(o_ref.dtype)


def _window_attention(p_rest, rope_cos, rope_sin, sink, n_ctx, seq):
    r = p_rest.shape[0]
    blk = C_BLOCK
    nblk = r // blk
    ncb = n_ctx // blk
    qw = C_QHEADS * C_HDIM
    kvw = C_KVHEADS * C_HDIM
    qc, kc, vc = REST_CQ // qw, REST_CK // kvw, REST_CV // kvw

    def prev(i):
        return jnp.maximum(i - 1, 0)

    def nxt(i):
        return jnp.minimum(i + 1, nblk - 1)

    def kv(col, f):
        return pl.BlockSpec((blk, kvw), lambda i: (f(i), col))

    def tab(f):
        return pl.BlockSpec((blk, C_HDIM), lambda i: (f(i), 0))

    same = lambda i: i
    return pl.pallas_call(
        functools.partial(_attn_kernel, n_ctx_blocks=ncb, seq=seq, n_ctx=n_ctx),
        out_shape=jax.ShapeDtypeStruct((r, qw), BF),
        grid=(nblk,),
        in_specs=[pl.BlockSpec(memory_space=pltpu.SMEM),
                  pl.BlockSpec((blk, qw), lambda i: (i, qc)),
                  kv(kc, prev), kv(kc, same), kv(kc, nxt),
                  kv(vc, prev), kv(vc, same), kv(vc, nxt),
                  pl.BlockSpec((n_ctx, kvw), lambda i: (0, kc)),
                  pl.BlockSpec((n_ctx, kvw), lambda i: (0, vc)),
                  tab(same), tab(same), tab(prev), tab(prev), tab(same), tab(same), tab(nxt), tab(nxt)],
        out_specs=pl.BlockSpec((blk, qw), lambda i: (i, 0)),
        compiler_params=_cparams(("arbitrary",)),
        name="window_attn",
    )(sink, p_rest, p_rest, p_rest, p_rest, p_rest, p_rest, p_rest, p_rest, p_rest,
      rope_cos, rope_sin, rope_cos, rope_sin, rope_cos, rope_sin, rope_cos, rope_sin)


def _rope_tables(seq, n_ctx):
    pos = jnp.arange(seq)
    rows = (pos // GRID_W).astype(F32)
    cols = (pos % GRID_W).astype(F32)
    half = C_HDIM // 2
    inv = ROPE_BASE ** (-jnp.arange(0, half, 2, dtype=F32) / half)
    ar, ac = rows[:, None] * inv, cols[:, None] * inv
    cos = jnp.concatenate([jnp.cos(ar), jnp.cos(ar), jnp.cos(ac), jnp.cos(ac)], axis=-1)
    sin = jnp.concatenate([-jnp.sin(ar), jnp.sin(ar), -jnp.sin(ac), jnp.sin(ac)], axis=-1)
    cos = jnp.concatenate([jnp.ones((n_ctx, C_HDIM), F32), cos], axis=0)
    sin = jnp.concatenate([jnp.zeros((n_ctx, C_HDIM), F32), sin], axis=0)
    return cos, sin


def _merge_kernel(ya_ref, yb_ref, yc_ref, ga_ref, gb_ref, gc_ref, wa_ref, wb_ref, wc_ref, o_ref):
    def branch(y_ref, g_ref, w_ref):
        return jax.nn.sigmoid(g_ref[...].astype(F32)) * jnp.dot(y_ref[...], w_ref[0],
                                                                preferred_element_type=F32)
    acc = branch(ya_ref, ga_ref, wa_ref) + branch(yb_ref, gb_ref, wb_ref) + branch(yc_ref, gc_ref, wc_ref)
    o_ref[...] = acc.astype(o_ref.dtype)


def _merge(ya, yb, yc, p_rest, w_a, w_b, w_c, layer):
    r = ya.shape[0]
    tm, tn = _row_tile(r), 512
    gb0 = REST_GATES // tn
    nper = D // tn

    def y_spec():
        return pl.BlockSpec((tm, ya.shape[1]), lambda n, i: (i, 0))

    def g_spec(br):
        return pl.BlockSpec((tm, tn), lambda n, i: (i, gb0 + br * nper + n))

    def w_spec():
        return pl.BlockSpec((1, ya.shape[1], tn), lambda n, i: (layer, 0, n))

    return pl.pallas_call(
        _merge_kernel,
        out_shape=jax.ShapeDtypeStruct((r, D), BF),
        grid=(D // tn, r // tm),
        in_specs=[y_spec(), y_spec(), y_spec(), g_spec(0), g_spec(1), g_spec(2), w_spec(), w_spec(), w_spec()],
        out_specs=pl.BlockSpec((tm, tn), lambda n, i: (i, n)),
        compiler_params=_cparams(("arbitrary", "arbitrary")),
        name="branch_merge",
    )(ya, yb, yc, p_rest, p_rest, p_rest, w_a, w_b, w_c)


def _out_proj_kernel(m_ref, w_ref, x_ref, mod_ref, g_ref, xo_ref, h_ref, *, n_ctx_blocks):
    is_ctx = pl.program_id(0) < n_ctx_blocks
    y = jnp.dot(m_ref[...], w_ref[0], preferred_element_type=F32)
    x = x_ref[...] + _mod_row(mod_ref, is_ctx, 2) * y
    xo_ref[...] = x
    h = x * lax.rsqrt(jnp.mean(x * x, axis=-1, keepdims=True) + EPS) * g_ref[...]
    h_ref[...] = h * (1.0 + _mod_row(mod_ref, is_ctx, 4)) + _mod_row(mod_ref, is_ctx, 3)


def _out_proj(merged, w_out, x, mod, g2, layer, n_ctx_blocks):
    r = x.shape[0]
    rb = ROW_BLOCK
    return pl.pallas_call(
        functools.partial(_out_proj_kernel, n_ctx_blocks=n_ctx_blocks),
        out_shape=(jax.ShapeDtypeStruct((r, D), F32), jax.ShapeDtypeStruct((r, D), F32)),
        grid=(r // rb,),
        in_specs=[pl.BlockSpec((rb, D), lambda i: (i, 0)),
                  pl.BlockSpec((1, D, D), lambda i: (layer, 0, 0)),
                  pl.BlockSpec((rb, D), lambda i: (i, 0)),
                  pl.BlockSpec((8, 6 * D), lambda i: (0, 0)),
                  pl.BlockSpec((1, D), lambda i: (0, 0))],
        out_specs=(pl.BlockSpec((rb, D), lambda i: (i, 0)), pl.BlockSpec((rb, D), lambda i: (i, 0))),
        compiler_params=_cparams(("arbitrary",)),
        name="out_proj",
    )(merged, w_out, x, mod, g2.reshape(1, D))


def _router_kernel(h_ref, w_ref, o_ref):
    logits = lax.dot_general(w_ref[...], h_ref[...], (((1,), (1,)), ((), ())),
                             precision=HI, preferred_element_type=F32)
    e = jnp.exp(logits - jnp.max(logits, axis=0, keepdims=True))
    o_ref[0] = e / jnp.sum(e, axis=0, keepdims=True)


def _router(h2, w_router_t):
    r = h2.shape[0]
    nblk = r // LANES
    return pl.pallas_call(
        _router_kernel,
        out_shape=jax.ShapeDtypeStruct((nblk, N_EXPERTS, LANES), F32),
        grid=(nblk,),
        in_specs=[pl.BlockSpec((LANES, D), lambda i: (i, 0)),
                  pl.BlockSpec((N_EXPERTS, D), lambda i: (0, 0))],
        out_specs=pl.BlockSpec((1, N_EXPERTS, LANES), lambda i: (i, 0, 0)),
        compiler_params=_cparams(("arbitrary",)),
        name="router",
    )(h2, w_router_t)


def _cumsum_tokens(m):
    nb = m.shape[0]
    m2 = m.reshape(nb * N_EXPERTS, LANES).astype(BF)
    li = lax.broadcasted_iota(jnp.int32, (LANES, LANES), 0)
    lj = lax.broadcasted_iota(jnp.int32, (LANES, LANES), 1)
    tri = jnp.where(li <= lj, 1.0, 0.0).astype(BF)
    ones = jnp.ones((LANES, LANES), BF)
    loc = jnp.dot(m2, tri, preferred_element_type=F32).reshape(nb, N_EXPERTS, LANES)
    tot = jnp.dot(m2, ones, preferred_element_type=F32).reshape(nb, N_EXPERTS, LANES)
    offs, run = [], jnp.zeros((N_EXPERTS, LANES), F32)
    for b in range(nb):
        offs.append(run)
        run = run + tot[b]
    return loc + jnp.stack(offs, axis=0)


def _select_kernel(aff_ref, idx_ref, gate_ref, slot_scr, acc_i, acc_g, *, sets):
    blocks_per_step = 2
    for b0, nb, cap, slot0 in sets:
        aff = aff_ref[b0:b0 + nb]
        bits = pltpu.bitcast(aff, jnp.int32)

        def count(mask):
            c = jnp.sum(jnp.where(mask, 1.0, 0.0), axis=0)
            return jnp.sum(c, axis=1, keepdims=True)

        def search(it, thr):
            cand = thr | jnp.left_shift(jnp.int32(1), 30 - it)
            return jnp.where(count(bits >= cand[None]) >= cap, cand, thr)

        thr = lax.fori_loop(0, 31, search, jnp.zeros((N_EXPERTS, 1), jnp.int32))[None]
        gt = bits > thr
        eq = bits == thr
        need = (cap - count(gt))[None]
        eq_rank = _cumsum_tokens(jnp.where(eq, 1.0, 0.0))
        sel = gt | (eq & (eq_rank <= need))
        slot = _cumsum_tokens(jnp.where(sel, 1.0, 0.0)) - 1.0 + slot0
        slot_scr[b0:b0 + nb] = jnp.where(sel, slot, -1.0)

        p_iota = lax.broadcasted_iota(jnp.int32, (cap, LANES), 0).astype(F32) + slot0
        lane = lax.broadcasted_iota(jnp.int32, (1, LANES), 1)

        def per_expert(e, carry):
            acc_i[0:cap] = jnp.zeros((cap, LANES), F32)
            acc_g[0:cap] = jnp.zeros((cap, LANES), F32)

            def per_blocks(bb, carry2):
                ai = acc_i[0:cap]
                ag = acc_g[0:cap]
                for u in range(blocks_per_step):
                    b = b0 + bb * blocks_per_step + u
                    hit = p_iota == slot_scr[b, pl.ds(e, 1), :]
                    tok = (b * LANES + lane).astype(F32)
                    ai = ai + jnp.where(hit, tok, 0.0)
                    ag = ag + jnp.where(hit, aff_ref[b, pl.ds(e, 1), :], 0.0)
                acc_i[0:cap] = ai
                acc_g[0:cap] = ag
                return carry2

            lax.fori_loop(0, nb // blocks_per_step, per_blocks, 0)
            idx_ref[e, slot0:slot0 + cap, :] = jnp.sum(acc_i[0:cap], axis=1, keepdims=True).astype(jnp.int32)
            gate_ref[e, slot0:slot0 + cap, :] = jnp.sum(acc_g[0:cap], axis=1, keepdims=True)
            return carry

        lax.fori_loop(0, N_EXPERTS, per_expert, 0)


def _select(aff, n_ctx, seq):
    nb_ctx, nb_lat = n_ctx // LANES, seq // LANES
    cap_ctx = EC_CAPACITY * n_ctx // N_EXPERTS
    cap_lat = EC_CAPACITY * seq // N_EXPERTS
    slots = cap_ctx + cap_lat
    sets = ((0, nb_ctx, cap_ctx, 0), (nb_ctx, nb_lat, cap_lat, cap_ctx))
    return pl.pallas_call(
        functools.partial(_select_kernel, sets=sets),
        out_shape=(jax.ShapeDtypeStruct((N_EXPERTS, slots, 1), jnp.int32),
                   jax.ShapeDtypeStruct((N_EXPERTS, slots, 1), F32)),
        scratch_shapes=[pltpu.VMEM(aff.shape, F32),
                        pltpu.VMEM((cap_lat, LANES), F32), pltpu.VMEM((cap_lat, LANES), F32)],
        compiler_params=pltpu.CompilerParams(vmem_limit_bytes=VMEM_LIMIT),
        name="expert_select",
    )(aff)


def _ffn_kernel(idx_ref, h_hbm, x_hbm, gate_ref, mod_ref, wg_ref, wu_ref, wd_ref, xo_hbm,
                xs, xsb, xr, acc, sem_h, sem_x, sem_s, *, slots, ctx_slots, nf, n_experts):
    del x_hbm
    e = pl.program_id(0)
    f = pl.program_id(1)
    base = e * slots

    def scatter_done():
        pltpu.make_async_copy(xr, xo_hbm.at[pl.ds(0, slots)], sem_s).wait()

    @pl.when(f == 0)
    def _():
        @pl.when(e > 0)
        def _():
            scatter_done()

        def issue(p, carry):
            row = idx_ref[base + p]
            pltpu.make_async_copy(h_hbm.at[pl.ds(row, 1)], xs.at[pl.ds(p, 1)], sem_h).start()
            pltpu.make_async_copy(xo_hbm.at[pl.ds(row, 1)], xr.at[pl.ds(p, 1)], sem_x).start()
            return carry

        lax.fori_loop(0, slots, issue, 0)
        pltpu.make_async_copy(h_hbm.at[pl.ds(0, slots)], xs, sem_h).wait()
        xsb[...] = xs[...].astype(BF)
        acc[...] = jnp.zeros_like(acc)

    xb = xsb[...]
    hid = _silu(jnp.dot(xb, wg_ref[0], preferred_element_type=F32)) * jnp.dot(
        xb, wu_ref[0], preferred_element_type=F32)
    acc[...] += jnp.dot(hid.astype(BF), wd_ref[0], preferred_element_type=F32)

    @pl.when(f == nf - 1)
    def _():
        pltpu.make_async_copy(xo_hbm.at[pl.ds(0, slots)], xr, sem_x).wait()
        is_ctx = lax.broadcasted_iota(jnp.int32, (slots, 1), 0) < ctx_slots
        m5 = jnp.where(is_ctx, mod_ref[1:2, 5 * D:6 * D], mod_ref[0:1, 5 * D:6 * D])
        xr[...] = xr[...] + m5 * (acc[...] * gate_ref[0])

        def issue(p, carry):
            row = idx_ref[base + p]
            pltpu.make_async_copy(xr.at[pl.ds(p, 1)], xo_hbm.at[pl.ds(row, 1)], sem_s).start()
            return carry

        lax.fori_loop(0, slots, issue, 0)

        @pl.when(e == n_experts - 1)
        def _():
            scatter_done()


def _expert_ffn(idx, h2, x, gate, mod, w_gate, w_up, w_down, layer, ctx_slots):
    n_exp, slots = idx.shape
    dff = w_gate.shape[-1]
    nf = dff // FFN_FCHUNK
    fc = FFN_FCHUNK
    grid_spec = pltpu.PrefetchScalarGridSpec(
        num_scalar_prefetch=1,
        grid=(n_exp, nf),
        in_specs=[pl.BlockSpec(memory_space=pl.ANY),
                  pl.BlockSpec(memory_space=pl.ANY),
                  pl.BlockSpec((1, slots, 1), lambda e, f, idx: (e, 0, 0)),
                  pl.BlockSpec((8, 6 * D), lambda e, f, idx: (0, 0)),
                  pl.BlockSpec((1, D, fc), lambda e, f, idx: (layer * n_exp + e, 0, f)),
                  pl.BlockSpec((1, D, fc), lambda e, f, idx: (layer * n_exp + e, 0, f)),
                  pl.BlockSpec((1, fc, D), lambda e, f, idx: (layer * n_exp + e, f, 0))],
        out_specs=pl.BlockSpec(memory_space=pl.ANY),
        scratch_shapes=[pltpu.VMEM((slots, D), F32), pltpu.VMEM((slots, D), BF),
                        pltpu.VMEM((slots, D), F32), pltpu.VMEM((slots, D), F32),
                        pltpu.SemaphoreType.DMA, pltpu.SemaphoreType.DMA, pltpu.SemaphoreType.DMA])
    return pl.pallas_call(
        functools.partial(_ffn_kernel, slots=slots, ctx_slots=ctx_slots, nf=nf, n_experts=n_exp),
        out_shape=jax.ShapeDtypeStruct(x.shape, F32),
        grid_spec=grid_spec,
        input_output_aliases={2: 0},
        compiler_params=_cparams(("arbitrary", "arbitrary")),
        name="expert_ffn",
    )(idx.reshape(-1), h2, x, gate, mod, w_gate, w_up, w_down)


def _final_norm_kernel(x_ref, g_ref, o_ref):
    x = x_ref[...]
    o_ref[...] = x * lax.rsqrt(jnp.mean(x * x, axis=-1, keepdims=True) + EPS) * g_ref[...]


def _final_norm(x, g, n_ctx_blocks):
    r = x.shape[0]
    rb = ROW_BLOCK
    nlat = r // rb - n_ctx_blocks
    return pl.pallas_call(
        _final_norm_kernel,
        out_shape=jax.ShapeDtypeStruct((nlat * rb, D), F32),
        grid=(nlat,),
        in_specs=[pl.BlockSpec((rb, D), lambda i: (i + n_ctx_blocks, 0)),
                  pl.BlockSpec((1, D), lambda i: (0, 0))],
        out_specs=pl.BlockSpec((rb, D), lambda i: (i, 0)),
        compiler_params=_cparams(("arbitrary",)),
        name="final_norm",
    )(x, g.reshape(1, D))


def _lower_bounds(lb_logits):
    p = jax.nn.softmax(lb_logits.astype(F32), axis=0)
    cs = jnp.cumsum(p, axis=0)
    return cs - cs[0]


def kernel(x, c, ctx, c_ctx, w_ada, b_ada, norm1_g, norm2_g, w_in, hgrn_lb_logits, hgrn_onorm_g,
           conv_w, conv_b, conv_ln_g, conv_ln_b, attn_sink, w_branch_a, w_branch_b, w_branch_c,
           w_out, w_router, w_exp_gate, w_exp_up, w_exp_down, final_norm_g):
    bsz, seq, _ = x.shape
    n_ctx = ctx.shape[1]
    depth = w_in.shape[0]
    assert bsz == 1 and x.shape[2] == D and w_in.shape[2] == D_IN
    assert n_ctx % ROW_BLOCK == 0 and seq % ROW_BLOCK == 0 and seq % GRID_W == 0
    ncb = n_ctx // ROW_BLOCK
    ctx_slots = EC_CAPACITY * n_ctx // N_EXPERTS

    xs = jnp.concatenate([ctx[0], x[0]], axis=0)
    c8 = jnp.zeros((8, D), F32).at[0].set(c[0]).at[1].set(c_ctx)
    mod = _modulation(c8, w_ada.astype(BF), b_ada)
    lbs = _lower_bounds(hgrn_lb_logits)
    rope_cos, rope_sin = _rope_tables(seq, n_ctx)

    w_in_b = w_in.astype(BF)
    w_a_b, w_b_b, w_c_b = w_branch_a.astype(BF), w_branch_b.astype(BF), w_branch_c.astype(BF)
    w_out_b = w_out.astype(BF)
    n_exp = w_exp_gate.shape[1]
    w_g_b = w_exp_gate.astype(BF).reshape(depth * n_exp, D, -1)
    w_u_b = w_exp_up.astype(BF).reshape(depth * n_exp, D, -1)
    w_d_b = w_exp_down.astype(BF).reshape(depth * n_exp, -1, D)

    for l in range(depth):
        hx = _norm_mod(xs, norm1_g[l], mod[l], 0, 1, ncb, BF)
        p_qv = _matmul_cols(hx, w_in_b, l, SEG_QV[0], SEG_QV[1] - SEG_QV[0], BF)
        p_z = _matmul_cols(hx, w_in_b, l, SEG_Z[0], SEG_Z[1] - SEG_Z[0], F32)
        p_rest = _matmul_cols(hx, w_in_b, l, SEG_REST0, REST_WIDTH, BF)

        o_f = _hgrn_forward(p_qv, p_z, lbs[l, 0])
        ya = _hgrn_backward(p_qv, p_z, p_rest, o_f, lbs[l, 1], hgrn_onorm_g[l], ncb)
        yb = _conv_module(p_rest, conv_w[l], conv_b[l], conv_ln_g[l], conv_ln_b[l], ncb)
        yc = _window_attention(p_rest, rope_cos, rope_sin, attn_sink[l], n_ctx, seq)

        merged = _merge(ya, yb, yc, p_rest, w_a_b, w_b_b, w_c_b, l)
        xs, h2 = _out_proj(merged, w_out_b, xs, mod[l], norm2_g[l], l, ncb)

        aff = _router(h2, w_router[l].T)
        idx, gate = _select(aff, n_ctx, seq)
        xs = _expert_ffn(idx.reshape(n_exp, -1), h2, xs, gate, mod[l], w_g_b, w_u_b, w_d_b, l, ctx_slots)

    return _final_norm(xs, final_norm_g, ncb)[None]
```

```python
import functools

import jax
import jax.numpy as jnp
import numpy as np
from jax import lax
from jax.experimental import pallas as pl
from jax.experimental.pallas import tpu as pltpu

F32 = jnp.float32
BF = jnp.bfloat16
HI = lax.Precision.HIGHEST

D = 2048
EPS = 1e-6
GRID_W = 64
ROPE_BASE = 10000.0

A_HEADS = 8
A_DIM = 128
A_WIDTH = A_HEADS * A_DIM
A_CHUNK = 64
A_SAFE_RANGE = 120.0

B_WIDTH = 1024
B_KSIZE = 31
B_HALO = 16

C_QHEADS = 8
C_KVHEADS = 2
C_GROUP = C_QHEADS // C_KVHEADS
C_HDIM = 128
C_BLOCK = 128

N_EXPERTS = 16
EC_CAPACITY = 2
EXPERT_DFF = 1024
FFN_FCHUNK = 256
ISSUE_UNROLL = 8

ROW_BLOCK = 256
LANES = 128
VMEM_LIMIT = 56 * 1024 * 1024

SEG_QV = (0, 2 * A_WIDTH)
SEG_Z = (2 * A_WIDTH, 4 * A_WIDTH)
SEG_REST0 = 4 * A_WIDTH
REST_G = 0
REST_CONV_A = A_WIDTH
REST_CONV_G = A_WIDTH + B_WIDTH
REST_CQ = A_WIDTH + 2 * B_WIDTH
REST_CK = REST_CQ + C_QHEADS * C_HDIM
REST_CV = REST_CK + C_KVHEADS * C_HDIM
REST_GATES = REST_CV + C_KVHEADS * C_HDIM
REST_WIDTH = REST_GATES + 3 * D
D_IN = SEG_REST0 + REST_WIDTH


def _cparams(sem, vmem=VMEM_LIMIT):
    return pltpu.CompilerParams(dimension_semantics=sem, vmem_limit_bytes=vmem)


def _silu(x):
    return x * jax.nn.sigmoid(x)


def _mod_kernel(c_ref, w_ref, b_ref, o_ref):
    s = _silu(c_ref[...]).astype(BF)
    o_ref[0] = jnp.dot(s, w_ref[0].astype(BF), preferred_element_type=F32) + b_ref[0]


def _modulation(c8, w_ada, b_ada):
    depth, _, n6 = w_ada.shape
    tn = D // 2
    return pl.pallas_call(
        _mod_kernel,
        out_shape=jax.ShapeDtypeStruct((depth, 8, n6), F32),
        grid=(depth, n6 // tn),
        in_specs=[pl.BlockSpec((8, D), lambda l, n: (0, 0)),
                  pl.BlockSpec((1, D, tn), lambda l, n: (l, 0, n)),
                  pl.BlockSpec((1, 1, tn), lambda l, n: (l, 0, n))],
        out_specs=pl.BlockSpec((1, 8, tn), lambda l, n: (l, 0, n)),
        compiler_params=_cparams(("arbitrary", "arbitrary")),
        name="adaln_mod",
    )(c8, w_ada, b_ada.reshape(depth, 1, n6))


def _mod_row(mod_ref, is_ctx, i):
    row = jnp.where(is_ctx, 1, 0)
    return mod_ref[pl.ds(row, 1), i * D:(i + 1) * D]


def _norm_mod_kernel(x_ref, g_ref, mod_ref, o_ref, *, shift_i, scale_i, n_ctx_blocks):
    x = x_ref[...]
    y = x * lax.rsqrt(jnp.mean(x * x, axis=-1, keepdims=True) + EPS) * g_ref[...]
    is_ctx = pl.program_id(0) < n_ctx_blocks
    y = y * (1.0 + _mod_row(mod_ref, is_ctx, scale_i)) + _mod_row(mod_ref, is_ctx, shift_i)
    o_ref[...] = y.astype(o_ref.dtype)


def _norm_mod(x, g, mod, shift_i, scale_i, n_ctx_blocks, out_dtype):
    r = x.shape[0]
    return pl.pallas_call(
        functools.partial(_norm_mod_kernel, shift_i=shift_i, scale_i=scale_i, n_ctx_blocks=n_ctx_blocks),
        out_shape=jax.ShapeDtypeStruct((r, D), out_dtype),
        grid=(r // ROW_BLOCK,),
        in_specs=[pl.BlockSpec((ROW_BLOCK, D), lambda i: (i, 0)),
                  pl.BlockSpec((1, D), lambda i: (0, 0)),
                  pl.BlockSpec((8, 6 * D), lambda i: (0, 0))],
        out_specs=pl.BlockSpec((ROW_BLOCK, D), lambda i: (i, 0)),
        compiler_params=_cparams(("arbitrary",)),
        name="norm_mod",
    )(x, g.reshape(1, D), mod)


def _mm_kernel(a_ref, b_ref, o_ref, bb_ref):
    @pl.when(pl.program_id(1) == 0)
    def _():
        bb_ref[...] = b_ref[0].astype(BF)

    o_ref[...] = jnp.dot(a_ref[...], bb_ref[...], preferred_element_type=F32).astype(o_ref.dtype)


def _row_tile(r, candidates=(1408, 768, ROW_BLOCK)):
    return next(t for t in candidates if r % t == 0)


def _matmul_cols(a, w3, layer, col0, ncols, out_dtype, tn):
    m, k = a.shape
    tm = _row_tile(m)
    assert col0 % tn == 0 and ncols % tn == 0
    cb0 = col0 // tn
    return pl.pallas_call(
        _mm_kernel,
        out_shape=jax.ShapeDtypeStruct((m, ncols), out_dtype),
        grid=(ncols // tn, m // tm),
        in_specs=[pl.BlockSpec((tm, k), lambda n, i: (i, 0)),
                  pl.BlockSpec((1, k, tn), lambda n, i: (layer, 0, cb0 + n))],
        out_specs=pl.BlockSpec((tm, tn), lambda n, i: (i, n)),
        scratch_shapes=[pltpu.VMEM((k, tn), BF)],
        compiler_params=_cparams(("arbitrary", "arbitrary")),
        name="in_proj",
    )(a, w3)


def _hgrn_kernel(*refs, backward, chunks):
    if backward:
        (q_ref, v_ref, z_ref, lb_ref, of_ref, g_ref, og_ref, out_ref,
         st_ref, a_scr, b_scr, k_scr, o_scr) = refs
    else:
        q_ref, v_ref, z_ref, lb_ref, out_ref, st_ref, a_scr, b_scr, k_scr = refs
        o_scr = out_ref
    cs = A_CHUNK

    @pl.when(pl.program_id(0) == 0)
    def _():
        st_ref[...] = jnp.zeros_like(st_ref)

    lb = lb_ref[...]
    z = z_ref[...]
    log_lb = jnp.log(lb)
    log_sig = jnp.minimum(z, 0.0) - jnp.log(1.0 + jnp.exp(-jnp.abs(z)))
    c = jnp.log(1.0 - lb) + log_sig
    lf = jnp.maximum(log_lb, c) + jnp.log(1.0 + jnp.exp(-jnp.abs(log_lb - c)))
    kk = (1.0 - lb) / (1.0 + jnp.exp(z))

    lo = None
    for ci in range(chunks):
        t = jnp.sum(lf[ci * cs:(ci + 1) * cs], axis=0, keepdims=True)
        lo = t if lo is None else jnp.minimum(lo, t)
    safe = jnp.min(lo) >= -A_SAFE_RANGE

    ri = lax.broadcasted_iota(jnp.int32, (cs, cs), 0)
    cj = lax.broadcasted_iota(jnp.int32, (cs, cs), 1)
    causal = (cj >= ri) if backward else (cj <= ri)
    tri = jnp.where(causal, 1.0, 0.0).astype(BF)
    lane_cs = lax.broadcasted_iota(jnp.int32, (1, cs), 1)
    hi = lax.broadcasted_iota(jnp.int32, (A_WIDTH, LANES), 0) // A_DIM
    hj = lax.broadcasted_iota(jnp.int32, (A_WIDTH, LANES), 1)
    head_ind = jnp.where(hi == hj, 1.0, 0.0).astype(F32)

    order = range(chunks - 1, -1, -1) if backward else range(chunks)
    for ci in order:
        rows = slice(ci * cs, (ci + 1) * cs)
        q_c = q_ref[rows, :].astype(F32)
        v_c = v_ref[rows, :]
        k_c = kk[rows]
        lf_c = lf[rows]
        p1 = lf_c.astype(BF)
        r1 = lf_c - p1.astype(F32)
        p2 = r1.astype(BF)
        p3 = (r1 - p2.astype(F32)).astype(BF)
        b = (jnp.dot(tri, p1, preferred_element_type=F32) + jnp.dot(tri, p2, preferred_element_type=F32)
             + jnp.dot(tri, p3, preferred_element_type=F32))
        b_lo, b_hi = b[0:1], b[cs - 1:cs]
        tot = b_lo if backward else b_hi

        @pl.when(safe)
        def _():
            r = 0.5 * (b_lo + b_hi)
            qa = (q_c * jnp.exp(b - r)).astype(BF)
            kb = (k_c * jnp.exp(r - b)).astype(BF)
            for h in range(A_HEADS):
                hs = slice(h * A_DIM, (h + 1) * A_DIM)
                a_scr[h] = lax.dot_general(qa[:, hs], kb[:, hs], (((1,), (1,)), ((), ())),
                                           preferred_element_type=F32)

        @pl.when(jnp.logical_not(safe))
        def _():
            b_scr[...] = b
            k_scr[...] = k_c

            def body(s, acc):
                rb = b_scr[pl.ds(s, 1), :]
                rk = k_scr[pl.ds(s, 1), :]
                p = q_c * jnp.exp(jnp.minimum(b - rb, 0.0)) * rk
                col = jnp.dot(p, head_ind, precision=HI, preferred_element_type=F32)
                onehot = jnp.where(lane_cs == s, 1.0, 0.0)
                return tuple(acc[h] + col[:, h:h + 1] * onehot for h in range(A_HEADS))

            acc = lax.fori_loop(0, cs, body, tuple(jnp.zeros((cs, cs), F32) for _ in range(A_HEADS)))
            for h in range(A_HEADS):
                a_scr[h] = acc[h]

        qe = (q_c * jnp.exp(b)).astype(BF)
        ke = (k_c * jnp.exp(tot - b)).astype(BF)
        et = jnp.exp(tot)
        for h in range(A_HEADS):
            hs = slice(h * A_DIM, (h + 1) * A_DIM)
            am = jnp.where(causal, a_scr[h], 0.0).astype(BF)
            st = st_ref[h]
            o_h = jnp.dot(am, v_c[:, hs], preferred_element_type=F32)
            o_h = o_h + lax.dot_general(qe[:, hs], st.astype(BF), (((1,), (1,)), ((), ())),
                                        preferred_element_type=F32)
            st_ref[h] = st * et[:, hs] + lax.dot_general(v_c[:, hs], ke[:, hs], (((0,), (0,)), ((), ())),
                                                         preferred_element_type=F32)
            o_scr[rows, hs] = o_h

    if backward:
        o = of_ref[...] + o_scr[...]
        gate = _silu(g_ref[...].astype(F32))
        og = og_ref[...]
        for h in range(A_HEADS):
            hs = slice(h * A_DIM, (h + 1) * A_DIM)
            o_h = o[:, hs]
            y = o_h * lax.rsqrt(jnp.mean(o_h * o_h, axis=-1, keepdims=True) + EPS) * og[:, hs]
            out_ref[:, hs] = (y * gate[:, hs]).astype(out_ref.dtype)


def _hgrn_scratch(rows, backward):
    s = [pltpu.VMEM((A_HEADS, A_DIM, A_DIM), F32),
         pltpu.VMEM((A_HEADS, A_CHUNK, A_CHUNK), F32),
         pltpu.VMEM((A_CHUNK, A_WIDTH), F32),
         pltpu.VMEM((A_CHUNK, A_WIDTH), F32)]
    if backward:
        s.append(pltpu.VMEM((rows, A_WIDTH), F32))
    return s


def _hgrn_forward(p_qv, p_z, lb_fwd):
    r = p_qv.shape[0]
    rb = ROW_BLOCK
    return pl.pallas_call(
        functools.partial(_hgrn_kernel, backward=False, chunks=rb // A_CHUNK),
        out_shape=jax.ShapeDtypeStruct((r, A_WIDTH), F32),
        grid=(r // rb,),
        in_specs=[pl.BlockSpec((rb, A_WIDTH), lambda i: (i, 0)),
                  pl.BlockSpec((rb, A_WIDTH), lambda i: (i, 1)),
                  pl.BlockSpec((rb, A_WIDTH), lambda i: (i, 0)),
                  pl.BlockSpec((1, A_WIDTH), lambda i: (0, 0))],
        out_specs=pl.BlockSpec((rb, A_WIDTH), lambda i: (i, 0)),
        scratch_shapes=_hgrn_scratch(rb, False),
        compiler_params=_cparams(("arbitrary",)),
        name="hgrn_fwd",
    )(p_qv, p_qv, p_z, lb_fwd.reshape(1, A_WIDTH))


def _hgrn_backward(p_qv, p_z, p_rest, o_f, lb_bwd, onorm_g, n_ctx_blocks):
    r = p_qv.shape[0]
    rb = ROW_BLOCK
    nblk = r // rb

    def blk(s):
        return jnp.where(s < n_ctx_blocks, n_ctx_blocks - 1 - s, nblk - 1 - (s - n_ctx_blocks))

    g_col = REST_G // A_WIDTH
    return pl.pallas_call(
        functools.partial(_hgrn_kernel, backward=True, chunks=rb // A_CHUNK),
        out_shape=jax.ShapeDtypeStruct((r, A_WIDTH), BF),
        grid=(nblk,),
        in_specs=[pl.BlockSpec((rb, A_WIDTH), lambda s: (blk(s), 0)),
                  pl.BlockSpec((rb, A_WIDTH), lambda s: (blk(s), 1)),
                  pl.BlockSpec((rb, A_WIDTH), lambda s: (blk(s), 1)),
                  pl.BlockSpec((1, A_WIDTH), lambda s: (0, 0)),
                  pl.BlockSpec((rb, A_WIDTH), lambda s: (blk(s), 0)),
                  pl.BlockSpec((rb, A_WIDTH), lambda s: (blk(s), g_col)),
                  pl.BlockSpec((1, A_WIDTH), lambda s: (0, 0))],
        out_specs=pl.BlockSpec((rb, A_WIDTH), lambda s: (blk(s), 0)),
        scratch_shapes=_hgrn_scratch(rb, True),
        compiler_params=_cparams(("arbitrary",)),
        name="hgrn_bwd",
    )(p_qv, p_qv, p_z, lb_bwd.reshape(1, A_WIDTH), o_f, p_rest, onorm_g.reshape(1, A_WIDTH))


def _conv_kernel(ap_ref, ac_ref, an_ref, gp_ref, gc_ref, gn_ref, w_ref, b_ref, lg_ref, lbias_ref,
                 o_ref, hext, acc_scr, *, nblk, n_ctx_blocks):
    i = pl.program_id(0)
    rb = ROW_BLOCK

    def glu(a_ref, g_ref):
        return a_ref[...].astype(F32) * jax.nn.sigmoid(g_ref[...].astype(F32))

    prev_ok = jnp.logical_and(i != 0, i != n_ctx_blocks)
    next_ok = jnp.logical_and(i != n_ctx_blocks - 1, i != nblk - 1)
    hext[0:B_HALO] = jnp.where(prev_ok, glu(ap_ref, gp_ref), 0.0)
    hext[B_HALO:B_HALO + rb] = glu(ac_ref, gc_ref)
    hext[B_HALO + rb:B_HALO + rb + B_HALO] = jnp.where(next_ok, glu(an_ref, gn_ref), 0.0)

    row0 = B_HALO - B_KSIZE // 2
    sub = 8
    rt, ct = 128, 128
    for cc in range(B_WIDTH // ct):
        cs = slice(cc * ct, (cc + 1) * ct)
        for rc in range(rb // rt):
            t0 = rc * rt
            acc = jnp.zeros((rt, ct), F32) + b_ref[:, cs]
            for s in range(sub):
                part = None
                for j in range(B_KSIZE):
                    if (row0 + j) % sub != s:
                        continue
                    start = t0 + row0 + j - s
                    term = w_ref[j:j + 1, cs] * hext[start:start + rt + sub, cs]
                    part = term if part is None else part + term
                if part is not None:
                    acc = acc + part[s:s + rt]
            acc_scr[t0:t0 + rt, cs] = acc

    h = acc_scr[...]
    hc = h - jnp.mean(h, axis=-1, keepdims=True)
    y = hc * lax.rsqrt(jnp.mean(hc * hc, axis=-1, keepdims=True) + EPS) * lg_ref[...] + lbias_ref[...]
    o_ref[...] = _silu(y).astype(o_ref.dtype)


def _conv_module(p_rest, conv_w, conv_b, ln_g, ln_b, n_ctx_blocks):
    r = p_rest.shape[0]
    rb = ROW_BLOCK
    nblk = r // rb
    hpb = rb // B_HALO
    nh = r // B_HALO
    ca, cg = REST_CONV_A // B_WIDTH, REST_CONV_G // B_WIDTH

    def prev(i):
        return jnp.maximum(i * hpb - 1, 0)

    def nxt(i):
        return jnp.minimum((i + 1) * hpb, nh - 1)

    vec = lambda: pl.BlockSpec((1, B_WIDTH), lambda i: (0, 0))
    return pl.pallas_call(
        functools.partial(_conv_kernel, nblk=nblk, n_ctx_blocks=n_ctx_blocks),
        out_shape=jax.ShapeDtypeStruct((r, B_WIDTH), BF),
        grid=(nblk,),
        in_specs=[pl.BlockSpec((B_HALO, B_WIDTH), lambda i: (prev(i), ca)),
                  pl.BlockSpec((rb, B_WIDTH), lambda i: (i, ca)),
                  pl.BlockSpec((B_HALO, B_WIDTH), lambda i: (nxt(i), ca)),
                  pl.BlockSpec((B_HALO, B_WIDTH), lambda i: (prev(i), cg)),
                  pl.BlockSpec((rb, B_WIDTH), lambda i: (i, cg)),
                  pl.BlockSpec((B_HALO, B_WIDTH), lambda i: (nxt(i), cg)),
                  pl.BlockSpec((B_KSIZE, B_WIDTH), lambda i: (0, 0)),
                  vec(), vec(), vec()],
        out_specs=pl.BlockSpec((rb, B_WIDTH), lambda i: (i, 0)),
        scratch_shapes=[pltpu.VMEM((rb + 2 * B_HALO, B_WIDTH), F32), pltpu.VMEM((rb, B_WIDTH), F32)],
        compiler_params=_cparams(("arbitrary",)),
        name="conv_module",
    )(p_rest, p_rest, p_rest, p_rest, p_rest, p_rest, conv_w,
      conv_b.reshape(1, B_WIDTH), ln_g.reshape(1, B_WIDTH), ln_b.reshape(1, B_WIDTH))


def _rope(x, cos, sin):
    lane = lax.broadcasted_iota(jnp.int32, x.shape, 1)
    quarter = C_HDIM // 4
    swapped = jnp.where((lane & (2 * quarter - 1)) < quarter,
                        pltpu.roll(x, C_HDIM - quarter, 1), pltpu.roll(x, quarter, 1))
    return x * cos + swapped * sin


def _attn_kernel(sink_ref, q_ref, kp_ref, kc_ref, kn_ref, vp_ref, vc_ref, vn_ref, ck_ref, cv_ref,
                 cq_ref, sq_ref, cp_ref, sp_ref, cc_ref, sc_ref, cn_ref, sn_ref, o_ref,
                 *, n_ctx_blocks, seq, n_ctx):
    i = pl.program_id(0)
    blk = C_BLOCK
    nloc = 3 * blk
    nq = C_GROUP * blk
    scale = C_HDIM ** -0.5

    row = lax.broadcasted_iota(jnp.int32, (nq, nloc), 0)
    col = lax.broadcasted_iota(jnp.int32, (nq, nloc), 1)
    t = row & (blk - 1)
    kpos = (i - n_ctx_blocks - 1) * blk + col
    valid = (col >= t) & (col <= t + 2 * blk) & (kpos >= 0) & (kpos < seq) & (i >= n_ctx_blocks)
    rgrp = lax.broadcasted_iota(jnp.int32, (nq, 1), 0) // blk

    cq, sq = cq_ref[...], sq_ref[...]
    for g in range(C_KVHEADS):
        ks = slice(g * C_HDIM, (g + 1) * C_HDIM)
        kparts = [_rope(kp_ref[:, ks].astype(F32), cp_ref[...], sp_ref[...]).astype(BF),
                  _rope(kc_ref[:, ks].astype(F32), cc_ref[...], sc_ref[...]).astype(BF),
                  _rope(kn_ref[:, ks].astype(F32), cn_ref[...], sn_ref[...]).astype(BF),
                  ck_ref[:, ks]]
        k_all = jnp.concatenate(kparts, axis=0)
        v_all = jnp.concatenate([vp_ref[:, ks], vc_ref[:, ks], vn_ref[:, ks], cv_ref[:, ks]], axis=0)
        qparts, sink_col = [], jnp.zeros((nq, 1), F32)
        for j in range(C_GROUP):
            h = g * C_GROUP + j
            qparts.append(_rope(q_ref[:, h * C_HDIM:(h + 1) * C_HDIM].astype(F32), cq, sq).astype(BF))
            sink_col = jnp.where(rgrp == j, sink_ref[h], sink_col)
        q_all = jnp.concatenate(qparts, axis=0)
        s = lax.dot_general(q_all, k_all, (((1,), (1,)), ((), ())), preferred_element_type=F32) * scale
        s_loc = jnp.where(valid, s[:, :nloc], -1e30)
        s_ctx = s[:, nloc:]
        m = jnp.maximum(jnp.maximum(jnp.max(s_loc, axis=-1, keepdims=True),
                                    jnp.max(s_ctx, axis=-1, keepdims=True)), sink_col)
        p_loc = jnp.exp(s_loc - m)
        p_ctx = jnp.exp(s_ctx - m)
        den = (jnp.sum(p_loc, axis=-1, keepdims=True) + jnp.sum(p_ctx, axis=-1, keepdims=True)
               + jnp.exp(sink_col - m))
        o = jnp.dot(p_loc.astype(BF), v_all[:nloc], preferred_element_type=F32)
        o = o + jnp.dot(p_ctx.astype(BF), v_all[nloc:], preferred_element_type=F32)
        o = o / den
        for j in range(C_GROUP):
            h = g * C_GROUP + j
            o_ref[:, h * C_HDIM:(h + 1) * C_HDIM] = o[j * blk:(j + 1) * blk].astype(o_ref.dtype)


def _window_attention(p_rest, rope_cos, rope_sin, sink, n_ctx, seq):
    r = p_rest.shape[0]
    blk = C_BLOCK
    nblk = r // blk
    ncb = n_ctx // blk
    qw = C_QHEADS * C_HDIM
    kvw = C_KVHEADS * C_HDIM
    qc, kc, vc = REST_CQ // qw, REST_CK // kvw, REST_CV // kvw

    def prev(i):
        return jnp.maximum(i - 1, 0)

    def nxt(i):
        return jnp.minimum(i + 1, nblk - 1)

    def kv(col, f):
        return pl.BlockSpec((blk, kvw), lambda i: (f(i), col))

    def tab(f):
        return pl.BlockSpec((blk, C_HDIM), lambda i: (f(i), 0))

    same = lambda i: i
    return pl.pallas_call(
        functools.partial(_attn_kernel, n_ctx_blocks=ncb, seq=seq, n_ctx=n_ctx),
        out_shape=jax.ShapeDtypeStruct((r, qw), BF),
        grid=(nblk,),
        in_specs=[pl.BlockSpec(memory_space=pltpu.SMEM),
                  pl.BlockSpec((blk, qw), lambda i: (i, qc)),
                  kv(kc, prev), kv(kc, same), kv(kc, nxt),
                  kv(vc, prev), kv(vc, same), kv(vc, nxt),
                  pl.BlockSpec((n_ctx, kvw), lambda i: (0, kc)),
                  pl.BlockSpec((n_ctx, kvw), lambda i: (0, vc)),
                  tab(same), tab(same), tab(prev), tab(prev), tab(same), tab(same), tab(nxt), tab(nxt)],
        out_specs=pl.BlockSpec((blk, qw), lambda i: (i, 0)),
        compiler_params=_cparams(("arbitrary",)),
        name="window_attn",
    )(sink, p_rest, p_rest, p_rest, p_rest, p_rest, p_rest, p_rest, p_rest, p_rest,
      rope_cos, rope_sin, rope_cos, rope_sin, rope_cos, rope_sin, rope_cos, rope_sin)


def _rope_tables(seq, n_ctx):
    pos = jnp.arange(seq)
    rows = (pos // GRID_W).astype(F32)
    cols = (pos % GRID_W).astype(F32)
    half = C_HDIM // 2
    inv = ROPE_BASE ** (-jnp.arange(0, half, 2, dtype=F32) / half)
    ar, ac = rows[:, None] * inv, cols[:, None] * inv
    cos = jnp.concatenate([jnp.cos(ar), jnp.cos(ar), jnp.cos(ac), jnp.cos(ac)], axis=-1)
    sin = jnp.concatenate([-jnp.sin(ar), jnp.sin(ar), -jnp.sin(ac), jnp.sin(ac)], axis=-1)
    cos = jnp.concatenate([jnp.ones((n_ctx, C_HDIM), F32), cos], axis=0)
    sin = jnp.concatenate([jnp.zeros((n_ctx, C_HDIM), F32), sin], axis=0)
    return cos, sin


def _merge_kernel(ya_ref, yb_ref, yc_ref, ga_ref, gb_ref, gc_ref, wa_ref, wb_ref, wc_ref, o_ref,
                  wab_ref, wbb_ref, wcb_ref):
    @pl.when(pl.program_id(1) == 0)
    def _():
        wab_ref[...] = wa_ref[0].astype(BF)
        wbb_ref[...] = wb_ref[0].astype(BF)
        wcb_ref[...] = wc_ref[0].astype(BF)

    def branch(y_ref, g_ref, w_ref):
        return jax.nn.sigmoid(g_ref[...].astype(F32)) * jnp.dot(y_ref[...], w_ref[...],
                                                                preferred_element_type=F32)
    acc = branch(ya_ref, ga_ref, wab_ref) + branch(yb_ref, gb_ref, wbb_ref) + branch(yc_ref, gc_ref, wcb_ref)
    o_ref[...] = acc.astype(o_ref.dtype)


def _merge(ya, yb, yc, p_rest, w_a, w_b, w_c, layer):
    r = ya.shape[0]
    tm, tn = _row_tile(r), 512
    gb0 = REST_GATES // tn
    nper = D // tn

    def y_spec():
        return pl.BlockSpec((tm, ya.shape[1]), lambda n, i: (i, 0))

    def g_spec(br):
        return pl.BlockSpec((tm, tn), lambda n, i: (i, gb0 + br * nper + n))

    def w_spec():
        return pl.BlockSpec((1, ya.shape[1], tn), lambda n, i: (layer, 0, n))

    return pl.pallas_call(
        _merge_kernel,
        out_shape=jax.ShapeDtypeStruct((r, D), BF),
        grid=(D // tn, r // tm),
        in_specs=[y_spec(), y_spec(), y_spec(), g_spec(0), g_spec(1), g_spec(2), w_spec(), w_spec(), w_spec()],
        out_specs=pl.BlockSpec((tm, tn), lambda n, i: (i, n)),
        scratch_shapes=[pltpu.VMEM((ya.shape[1], tn), BF) for _ in range(3)],
        compiler_params=_cparams(("arbitrary", "arbitrary")),
        name="branch_merge",
    )(ya, yb, yc, p_rest, p_rest, p_rest, w_a, w_b, w_c)


def _out_proj_kernel(m_ref, w_ref, x_ref, mod_ref, g_ref, xo_ref, h_ref, *, n_ctx_blocks):
    is_ctx = pl.program_id(0) < n_ctx_blocks
    y = jnp.dot(m_ref[...], w_ref[0], preferred_element_type=F32)
    x = x_ref[...] + _mod_row(mod_ref, is_ctx, 2) * y
    xo_ref[...] = x
    h = x * lax.rsqrt(jnp.mean(x * x, axis=-1, keepdims=True) + EPS) * g_ref[...]
    h_ref[...] = h * (1.0 + _mod_row(mod_ref, is_ctx, 4)) + _mod_row(mod_ref, is_ctx, 3)


def _out_proj(merged, w_out, x, mod, g2, layer, n_ctx_blocks):
    r = x.shape[0]
    rb = ROW_BLOCK
    return pl.pallas_call(
        functools.partial(_out_proj_kernel, n_ctx_blocks=n_ctx_blocks),
        out_shape=(jax.ShapeDtypeStruct((r, D), F32), jax.ShapeDtypeStruct((r, D), F32)),
        grid=(r // rb,),
        in_specs=[pl.BlockSpec((rb, D), lambda i: (i, 0)),
                  pl.BlockSpec((1, D, D), lambda i: (layer, 0, 0)),
                  pl.BlockSpec((rb, D), lambda i: (i, 0)),
                  pl.BlockSpec((8, 6 * D), lambda i: (0, 0)),
                  pl.BlockSpec((1, D), lambda i: (0, 0))],
        out_specs=(pl.BlockSpec((rb, D), lambda i: (i, 0)), pl.BlockSpec((rb, D), lambda i: (i, 0))),
        compiler_params=_cparams(("arbitrary",)),
        name="out_proj",
    )(merged, w_out, x, mod, g2.reshape(1, D))


def _router_kernel(h_ref, w_ref, o_ref):
    logits = lax.dot_general(w_ref[...], h_ref[...], (((1,), (1,)), ((), ())),
                             precision=HI, preferred_element_type=F32)
    e = jnp.exp(logits - jnp.max(logits, axis=0, keepdims=True))
    o_ref[0] = e / jnp.sum(e, axis=0, keepdims=True)


def _router(h2, w_router_t):
    r = h2.shape[0]
    nblk = r // LANES
    return pl.pallas_call(
        _router_kernel,
        out_shape=jax.ShapeDtypeStruct((nblk, N_EXPERTS, LANES), F32),
        grid=(nblk,),
        in_specs=[pl.BlockSpec((LANES, D), lambda i: (i, 0)),
                  pl.BlockSpec((N_EXPERTS, D), lambda i: (0, 0))],
        out_specs=pl.BlockSpec((1, N_EXPERTS, LANES), lambda i: (i, 0, 0)),
        compiler_params=_cparams(("arbitrary",)),
        name="router",
    )(h2, w_router_t)


def _cumsum_tokens(m):
    nb = m.shape[0]
    m2 = m.reshape(nb * N_EXPERTS, LANES).astype(BF)
    li = lax.broadcasted_iota(jnp.int32, (LANES, LANES), 0)
    lj = lax.broadcasted_iota(jnp.int32, (LANES, LANES), 1)
    tri = jnp.where(li <= lj, 1.0, 0.0).astype(BF)
    ones = jnp.ones((LANES, LANES), BF)
    loc = jnp.dot(m2, tri, preferred_element_type=F32).reshape(nb, N_EXPERTS, LANES)
    tot = jnp.dot(m2, ones, preferred_element_type=F32).reshape(nb, N_EXPERTS, LANES)
    offs, run = [], jnp.zeros((N_EXPERTS, LANES), F32)
    for b in range(nb):
        offs.append(run)
        run = run + tot[b]
    return loc + jnp.stack(offs, axis=0)


def _select_kernel(aff_ref, idx_ref, gate_ref, slot_scr, acc_i, acc_g, *, sets):
    blocks_per_step = 2
    for b0, nb, cap, slot0 in sets:
        aff = aff_ref[b0:b0 + nb]
        bits = pltpu.bitcast(aff, jnp.int32)

        def count(mask):
            c = jnp.sum(jnp.where(mask, 1.0, 0.0), axis=0)
            return jnp.sum(c, axis=1, keepdims=True)

        def search(it, thr):
            cand = thr | jnp.left_shift(jnp.int32(1), 30 - it)
            return jnp.where(count(bits >= cand[None]) >= cap, cand, thr)

        thr = lax.fori_loop(0, 31, search, jnp.zeros((N_EXPERTS, 1), jnp.int32))[None]
        gt = bits > thr
        eq = bits == thr
        need = (cap - count(gt))[None]
        eq_rank = _cumsum_tokens(jnp.where(eq, 1.0, 0.0))
        sel = gt | (eq & (eq_rank <= need))
        slot = _cumsum_tokens(jnp.where(sel, 1.0, 0.0)) - 1.0 + slot0
        slot_scr[b0:b0 + nb] = jnp.where(sel, slot, -1.0)

        p_iota = lax.broadcasted_iota(jnp.int32, (cap, LANES), 0).astype(F32) + slot0
        lane = lax.broadcasted_iota(jnp.int32, (1, LANES), 1)

        def per_expert(e, carry):
            acc_i[0:cap] = jnp.zeros((cap, LANES), F32)
            acc_g[0:cap] = jnp.zeros((cap, LANES), F32)

            def per_blocks(bb, carry2):
                ai = acc_i[0:cap]
                ag = acc_g[0:cap]
                for u in range(blocks_per_step):
                    b = b0 + bb * blocks_per_step + u
                    hit = p_iota == slot_scr[b, pl.ds(e, 1), :]
                    tok = (b * LANES + lane).astype(F32)
                    ai = ai + jnp.where(hit, tok, 0.0)
                    ag = ag + jnp.where(hit, aff_ref[b, pl.ds(e, 1), :], 0.0)
                acc_i[0:cap] = ai
                acc_g[0:cap] = ag
                return carry2

            lax.fori_loop(0, nb // blocks_per_step, per_blocks, 0)
            idx_ref[e, slot0:slot0 + cap, :] = jnp.sum(acc_i[0:cap], axis=1, keepdims=True).astype(jnp.int32)
            gate_ref[e, slot0:slot0 + cap, :] = jnp.sum(acc_g[0:cap], axis=1, keepdims=True)
            return carry

        lax.fori_loop(0, N_EXPERTS, per_expert, 0)


def _select(aff, n_ctx, seq):
    nb_ctx, nb_lat = n_ctx // LANES, seq // LANES
    cap_ctx = EC_CAPACITY * n_ctx // N_EXPERTS
    cap_lat = EC_CAPACITY * seq // N_EXPERTS
    slots = cap_ctx + cap_lat
    sets = ((0, nb_ctx, cap_ctx, 0), (nb_ctx, nb_lat, cap_lat, cap_ctx))
    return pl.pallas_call(
        functools.partial(_select_kernel, sets=sets),
        out_shape=(jax.ShapeDtypeStruct((N_EXPERTS, slots, 1), jnp.int32),
                   jax.ShapeDtypeStruct((N_EXPERTS, slots, 1), F32)),
        scratch_shapes=[pltpu.VMEM(aff.shape, F32),
                        pltpu.VMEM((cap_lat, LANES), F32), pltpu.VMEM((cap_lat, LANES), F32)],
        compiler_params=pltpu.CompilerParams(vmem_limit_bytes=VMEM_LIMIT),
        name="expert_select",
    )(aff)


def _ffn_kernel(idx_ref, h_hbm, x_hbm, gate_ref, mod_ref, wg_ref, wu_ref, wd_ref, xo_hbm,
                xs, xsb, xr, acc, sem_h, sem_x, sem_s, *, slots, ctx_slots, nf, n_experts):
    del x_hbm
    e = pl.program_id(0)
    f = pl.program_id(1)
    base = e * slots

    def scatter_done():
        pltpu.make_async_copy(xr, xo_hbm.at[pl.ds(0, slots)], sem_s).wait()

    @pl.when(f == 0)
    def _():
        @pl.when(e > 0)
        def _():
            scatter_done()

        def issue(grp, carry):
            p0 = pl.multiple_of(grp * ISSUE_UNROLL, ISSUE_UNROLL)
            for u in range(ISSUE_UNROLL):
                row = idx_ref[base + p0 + u]
                pltpu.make_async_copy(h_hbm.at[pl.ds(row, 1)], xs.at[pl.ds(p0 + u, 1)], sem_h).start()
                pltpu.make_async_copy(xo_hbm.at[pl.ds(row, 1)], xr.at[pl.ds(p0 + u, 1)], sem_x).start()
            return carry

        lax.fori_loop(0, slots // ISSUE_UNROLL, issue, 0)
        pltpu.make_async_copy(h_hbm.at[pl.ds(0, slots)], xs, sem_h).wait()
        xsb[...] = xs[...].astype(BF)
        acc[...] = jnp.zeros_like(acc)

    xb = xsb[...]
    hid = _silu(jnp.dot(xb, wg_ref[0].astype(BF), preferred_element_type=F32)) * jnp.dot(
        xb, wu_ref[0].astype(BF), preferred_element_type=F32)
    acc[...] += jnp.dot(hid.astype(BF), wd_ref[0].astype(BF), preferred_element_type=F32)

    @pl.when(f == nf - 1)
    def _():
        pltpu.make_async_copy(xo_hbm.at[pl.ds(0, slots)], xr, sem_x).wait()
        is_ctx = lax.broadcasted_iota(jnp.int32, (slots, 1), 0) < ctx_slots
        m5 = jnp.where(is_ctx, mod_ref[1:2, 5 * D:6 * D], mod_ref[0:1, 5 * D:6 * D])
        xr[...] = xr[...] + m5 * (acc[...] * gate_ref[0])

        def issue(grp, carry):
            p0 = pl.multiple_of(grp * ISSUE_UNROLL, ISSUE_UNROLL)
            for u in range(ISSUE_UNROLL):
                row = idx_ref[base + p0 + u]
                pltpu.make_async_copy(xr.at[pl.ds(p0 + u, 1)], xo_hbm.at[pl.ds(row, 1)], sem_s).start()
            return carry

        lax.fori_loop(0, slots // ISSUE_UNROLL, issue, 0)

        @pl.when(e == n_experts - 1)
        def _():
            scatter_done()


def _expert_ffn(idx, h2, x, gate, mod, w_gate, w_up, w_down, layer, ctx_slots):
    n_exp, slots = idx.shape
    dff = w_gate.shape[-1]
    nf = dff // FFN_FCHUNK
    fc = FFN_FCHUNK
    grid_spec = pltpu.PrefetchScalarGridSpec(
        num_scalar_prefetch=1,
        grid=(n_exp, nf),
        in_specs=[pl.BlockSpec(memory_space=pl.ANY),
                  pl.BlockSpec(memory_space=pl.ANY),
                  pl.BlockSpec((1, slots, 1), lambda e, f, idx: (e, 0, 0)),
                  pl.BlockSpec((8, 6 * D), lambda e, f, idx: (0, 0)),
                  pl.BlockSpec((1, D, fc), lambda e, f, idx: (layer * n_exp + e, 0, f)),
                  pl.BlockSpec((1, D, fc), lambda e, f, idx: (layer * n_exp + e, 0, f)),
                  pl.BlockSpec((1, fc, D), lambda e, f, idx: (layer * n_exp + e, f, 0))],
        out_specs=pl.BlockSpec(memory_space=pl.ANY),
        scratch_shapes=[pltpu.VMEM((slots, D), F32), pltpu.VMEM((slots, D), BF),
                        pltpu.VMEM((slots, D), F32), pltpu.VMEM((slots, D), F32),
                        pltpu.SemaphoreType.DMA, pltpu.SemaphoreType.DMA, pltpu.SemaphoreType.DMA])
    return pl.pallas_call(
        functools.partial(_ffn_kernel, slots=slots, ctx_slots=ctx_slots, nf=nf, n_experts=n_exp),
        out_shape=jax.ShapeDtypeStruct(x.shape, F32),
        grid_spec=grid_spec,
        input_output_aliases={2: 0},
        compiler_params=_cparams(("arbitrary", "arbitrary")),
        name="expert_ffn",
    )(idx.reshape(-1), h2, x, gate, mod, w_gate, w_up, w_down)


def _final_norm_kernel(x_ref, g_ref, o_ref):
    x = x_ref[...]
    o_ref[...] = x * lax.rsqrt(jnp.mean(x * x, axis=-1, keepdims=True) + EPS) * g_ref[...]


def _final_norm(x, g, n_ctx_blocks):
    r = x.shape[0]
    rb = ROW_BLOCK
    nlat = r // rb - n_ctx_blocks
    return pl.pallas_call(
        _final_norm_kernel,
        out_shape=jax.ShapeDtypeStruct((nlat * rb, D), F32),
        grid=(nlat,),
        in_specs=[pl.BlockSpec((rb, D), lambda i: (i + n_ctx_blocks, 0)),
                  pl.BlockSpec((1, D), lambda i: (0, 0))],
        out_specs=pl.BlockSpec((rb, D), lambda i: (i, 0)),
        compiler_params=_cparams(("arbitrary",)),
        name="final_norm",
    )(x, g.reshape(1, D))


def _lower_bounds(lb_logits):
    p = jax.nn.softmax(lb_logits.astype(F32), axis=0)
    cs = jnp.cumsum(p, axis=0)
    return cs - cs[0]


def kernel(x, c, ctx, c_ctx, w_ada, b_ada, norm1_g, norm2_g, w_in, hgrn_lb_logits, hgrn_onorm_g,
           conv_w, conv_b, conv_ln_g, conv_ln_b, attn_sink, w_branch_a, w_branch_b, w_branch_c,
           w_out, w_router, w_exp_gate, w_exp_up, w_exp_down, final_norm_g):
    bsz, seq, _ = x.shape
    n_ctx = ctx.shape[1]
    depth = w_in.shape[0]
    assert bsz == 1 and x.shape[2] == D and w_in.shape[2] == D_IN
    assert n_ctx % ROW_BLOCK == 0 and seq % ROW_BLOCK == 0 and seq % GRID_W == 0
    ncb = n_ctx // ROW_BLOCK
    ctx_slots = EC_CAPACITY * n_ctx // N_EXPERTS

    xs = jnp.concatenate([ctx[0], x[0]], axis=0)
    c8 = jnp.zeros((8, D), F32).at[0].set(c[0]).at[1].set(c_ctx)
    mod = _modulation(c8, w_ada, b_ada)
    lbs = _lower_bounds(hgrn_lb_logits)
    rope_cos, rope_sin = _rope_tables(seq, n_ctx)

    w_a_b, w_b_b, w_c_b = w_branch_a, w_branch_b, w_branch_c
    w_out_b = w_out.astype(BF)
    n_exp = w_exp_gate.shape[1]
    w_g_b = w_exp_gate.reshape(depth * n_exp, D, -1)
    w_u_b = w_exp_up.reshape(depth * n_exp, D, -1)
    w_d_b = w_exp_down.reshape(depth * n_exp, -1, D)

    for l in range(depth):
        hx = _norm_mod(xs, norm1_g[l], mod[l], 0, 1, ncb, BF)
        p_qv = _matmul_cols(hx, w_in, l, SEG_QV[0], SEG_QV[1] - SEG_QV[0], BF, tn=1024)
        p_z = _matmul_cols(hx, w_in, l, SEG_Z[0], SEG_Z[1] - SEG_Z[0], F32, tn=1024)
        p_rest = _matmul_cols(hx, w_in, l, SEG_REST0, REST_WIDTH, BF, tn=512)

        o_f = _hgrn_forward(p_qv, p_z, lbs[l, 0])
        ya = _hgrn_backward(p_qv, p_z, p_rest, o_f, lbs[l, 1], hgrn_onorm_g[l], ncb)
        yb = _conv_module(p_rest, conv_w[l], conv_b[l], conv_ln_g[l], conv_ln_b[l], ncb)
        yc = _window_attention(p_rest, rope_cos, rope_sin, attn_sink[l], n_ctx, seq)

        merged = _merge(ya, yb, yc, p_rest, w_a_b, w_b_b, w_c_b, l)
        xs, h2 = _out_proj(merged, w_out_b, xs, mod[l], norm2_g[l], l, ncb)

        aff = _router(h2, w_router[l].T)
        idx, gate = _select(aff, n_ctx, seq)
        xs = _expert_ffn(idx.reshape(n_exp, -1), h2, xs, gate, mod[l], w_g_b, w_u_b, w_d_b, l, ctx_slots)

    return _final_norm(xs, final_norm_g, ncb)[None]
```

```python
import functools

import jax
import jax.numpy as jnp
import numpy as np
from jax import lax
from jax.experimental import pallas as pl
from jax.experimental.pallas import tpu as pltpu

F32 = jnp.float32
BF = jnp.bfloat16
HI = lax.Precision.HIGHEST

D = 2048
EPS = 1e-6
GRID_W = 64
ROPE_BASE = 10000.0

A_HEADS = 8
A_DIM = 128
A_WIDTH = A_HEADS * A_DIM
A_CHUNK = 64
A_SAFE_RANGE = 120.0

B_WIDTH = 1024
B_KSIZE = 31
B_HALO = 16

C_QHEADS = 8
C_KVHEADS = 2
C_GROUP = C_QHEADS // C_KVHEADS
C_HDIM = 128
C_BLOCK = 128

N_EXPERTS = 16
EC_CAPACITY = 2
EXPERT_DFF = 1024
FFN_FCHUNK = 256
ISSUE_UNROLL = 8

ROW_BLOCK = 256
LANES = 128
VMEM_LIMIT = 56 * 1024 * 1024

SEG_QV = (0, 2 * A_WIDTH)
SEG_Z = (2 * A_WIDTH, 4 * A_WIDTH)
SEG_REST0 = 4 * A_WIDTH
REST_G = 0
REST_CONV_A = A_WIDTH
REST_CONV_G = A_WIDTH + B_WIDTH
REST_CQ = A_WIDTH + 2 * B_WIDTH
REST_CK = REST_CQ + C_QHEADS * C_HDIM
REST_CV = REST_CK + C_KVHEADS * C_HDIM
REST_GATES = REST_CV + C_KVHEADS * C_HDIM
REST_WIDTH = REST_GATES + 3 * D
D_IN = SEG_REST0 + REST_WIDTH


def _cparams(sem, vmem=VMEM_LIMIT):
    return pltpu.CompilerParams(dimension_semantics=sem, vmem_limit_bytes=vmem)


def _silu(x):
    return x * jax.nn.sigmoid(x)


def _mod_kernel(c_ref, w_ref, b_ref, o_ref):
    s = _silu(c_ref[...]).astype(BF)
    o_ref[0] = jnp.dot(s, w_ref[0].astype(BF), preferred_element_type=F32) + b_ref[0]


def _modulation(c8, w_ada, b_ada):
    depth, _, n6 = w_ada.shape
    tn = D // 2
    return pl.pallas_call(
        _mod_kernel,
        out_shape=jax.ShapeDtypeStruct((depth, 8, n6), F32),
        grid=(depth, n6 // tn),
        in_specs=[pl.BlockSpec((8, D), lambda l, n: (0, 0)),
                  pl.BlockSpec((1, D, tn), lambda l, n: (l, 0, n)),
                  pl.BlockSpec((1, 1, tn), lambda l, n: (l, 0, n))],
        out_specs=pl.BlockSpec((1, 8, tn), lambda l, n: (l, 0, n)),
        compiler_params=_cparams(("arbitrary", "arbitrary")),
        name="adaln_mod",
    )(c8, w_ada, b_ada.reshape(depth, 1, n6))


def _mod_row(mod_ref, is_ctx, i):
    row = jnp.where(is_ctx, 1, 0)
    return mod_ref[pl.ds(row, 1), i * D:(i + 1) * D]


def _norm_mod_kernel(x_ref, g_ref, mod_ref, o_ref, *, shift_i, scale_i, n_ctx_blocks):
    x = x_ref[...]
    y = x * lax.rsqrt(jnp.mean(x * x, axis=-1, keepdims=True) + EPS) * g_ref[...]
    is_ctx = pl.program_id(0) < n_ctx_blocks
    y = y * (1.0 + _mod_row(mod_ref, is_ctx, scale_i)) + _mod_row(mod_ref, is_ctx, shift_i)
    o_ref[...] = y.astype(o_ref.dtype)


def _norm_mod(x, g, mod, shift_i, scale_i, n_ctx_blocks, out_dtype):
    r = x.shape[0]
    return pl.pallas_call(
        functools.partial(_norm_mod_kernel, shift_i=shift_i, scale_i=scale_i, n_ctx_blocks=n_ctx_blocks),
        out_shape=jax.ShapeDtypeStruct((r, D), out_dtype),
        grid=(r // ROW_BLOCK,),
        in_specs=[pl.BlockSpec((ROW_BLOCK, D), lambda i: (i, 0)),
                  pl.BlockSpec((1, D), lambda i: (0, 0)),
                  pl.BlockSpec((8, 6 * D), lambda i: (0, 0))],
        out_specs=pl.BlockSpec((ROW_BLOCK, D), lambda i: (i, 0)),
        compiler_params=_cparams(("arbitrary",)),
        name="norm_mod",
    )(x, g.reshape(1, D), mod)


def _mm_kernel(a_ref, b_ref, o_ref, bb_ref):
    @pl.when(pl.program_id(1) == 0)
    def _():
        bb_ref[...] = b_ref[0].astype(BF)

    o_ref[...] = jnp.dot(a_ref[...], bb_ref[...], preferred_element_type=F32).astype(o_ref.dtype)


def _row_tile(r, candidates=(1408, 768, ROW_BLOCK)):
    return next(t for t in candidates if r % t == 0)


def _matmul_cols(a, w3, layer, col0, ncols, out_dtype, tn):
    m, k = a.shape
    tm = _row_tile(m)
    assert col0 % tn == 0 and ncols % tn == 0
    cb0 = col0 // tn
    return pl.pallas_call(
        _mm_kernel,
        out_shape=jax.ShapeDtypeStruct((m, ncols), out_dtype),
        grid=(ncols // tn, m // tm),
        in_specs=[pl.BlockSpec((tm, k), lambda n, i: (i, 0)),
                  pl.BlockSpec((1, k, tn), lambda n, i: (layer, 0, cb0 + n))],
        out_specs=pl.BlockSpec((tm, tn), lambda n, i: (i, n)),
        scratch_shapes=[pltpu.VMEM((k, tn), BF)],
        compiler_params=_cparams(("arbitrary", "arbitrary")),
        name="in_proj",
    )(a, w3)


def _hgrn_kernel(*refs, backward, chunks):
    if backward:
        (q_ref, v_ref, z_ref, lb_ref, of_ref, g_ref, og_ref, out_ref,
         st_ref, b_scr, k_scr, o_scr) = refs
    else:
        q_ref, v_ref, z_ref, lb_ref, out_ref, st_ref, b_scr, k_scr = refs
        o_scr = out_ref
    cs = A_CHUNK

    @pl.when(pl.program_id(0) == 0)
    def _():
        st_ref[...] = jnp.zeros_like(st_ref)

    lb = lb_ref[...]
    z = z_ref[...]
    log_lb = jnp.log(lb)
    log_sig = jnp.minimum(z, 0.0) - jnp.log(1.0 + jnp.exp(-jnp.abs(z)))
    c = jnp.log(1.0 - lb) + log_sig
    lf = jnp.maximum(log_lb, c) + jnp.log(1.0 + jnp.exp(-jnp.abs(log_lb - c)))
    kk = (1.0 - lb) / (1.0 + jnp.exp(z))

    lo = None
    for ci in range(chunks):
        t = jnp.sum(lf[ci * cs:(ci + 1) * cs], axis=0, keepdims=True)
        lo = t if lo is None else jnp.minimum(lo, t)
    safe = jnp.min(lo) >= -A_SAFE_RANGE

    ri = lax.broadcasted_iota(jnp.int32, (cs, cs), 0)
    cj = lax.broadcasted_iota(jnp.int32, (cs, cs), 1)
    causal = (cj >= ri) if backward else (cj <= ri)
    tri = jnp.where(causal, 1.0, 0.0).astype(BF)
    lane_cs = lax.broadcasted_iota(jnp.int32, (1, cs), 1)
    hi = lax.broadcasted_iota(jnp.int32, (A_WIDTH, LANES), 0) // A_DIM
    hj = lax.broadcasted_iota(jnp.int32, (A_WIDTH, LANES), 1)
    head_ind = jnp.where(hi == hj, 1.0, 0.0).astype(F32)

    order = range(chunks - 1, -1, -1) if backward else range(chunks)

    def run_chunks(pairwise):
        for ci in order:
            rows = slice(ci * cs, (ci + 1) * cs)
            q_c = q_ref[rows, :].astype(F32)
            v_c = v_ref[rows, :]
            k_c = kk[rows]
            lf_c = lf[rows]
            p1 = lf_c.astype(BF)
            r1 = lf_c - p1.astype(F32)
            p2 = r1.astype(BF)
            p3 = (r1 - p2.astype(F32)).astype(BF)
            b = (jnp.dot(tri, p1, preferred_element_type=F32) + jnp.dot(tri, p2, preferred_element_type=F32)
                 + jnp.dot(tri, p3, preferred_element_type=F32))
            b_lo, b_hi = b[0:1], b[cs - 1:cs]
            tot = b_lo if backward else b_hi

            if pairwise:
                b_scr[...] = b
                k_scr[...] = k_c

                def body(s, acc, b=b, q_c=q_c):
                    rb = b_scr[pl.ds(s, 1), :]
                    rk = k_scr[pl.ds(s, 1), :]
                    p = q_c * jnp.exp(jnp.minimum(b - rb, 0.0)) * rk
                    col = jnp.dot(p, head_ind, precision=HI, preferred_element_type=F32)
                    onehot = jnp.where(lane_cs == s, 1.0, 0.0)
                    return tuple(acc[h] + col[:, h:h + 1] * onehot for h in range(A_HEADS))

                intra = lax.fori_loop(0, cs, body, tuple(jnp.zeros((cs, cs), F32) for _ in range(A_HEADS)))
            else:
                r = 0.5 * (b_lo + b_hi)
                qa = (q_c * jnp.exp(b - r)).astype(BF)
                kb = (k_c * jnp.exp(r - b)).astype(BF)
                intra = [lax.dot_general(qa[:, h * A_DIM:(h + 1) * A_DIM], kb[:, h * A_DIM:(h + 1) * A_DIM],
                                         (((1,), (1,)), ((), ())), preferred_element_type=F32)
                         for h in range(A_HEADS)]

            qe = (q_c * jnp.exp(b)).astype(BF)
            ke = (k_c * jnp.exp(tot - b)).astype(BF)
            et = jnp.exp(tot)
            for h in range(A_HEADS):
                hs = slice(h * A_DIM, (h + 1) * A_DIM)
                am = jnp.where(causal, intra[h], 0.0).astype(BF)
                st = st_ref[h]
                o_h = jnp.dot(am, v_c[:, hs], preferred_element_type=F32)
                o_h = o_h + lax.dot_general(qe[:, hs], st.astype(BF), (((1,), (1,)), ((), ())),
                                            preferred_element_type=F32)
                st_ref[h] = st * et[:, hs] + lax.dot_general(v_c[:, hs], ke[:, hs], (((0,), (0,)), ((), ())),
                                                             preferred_element_type=F32)
                o_scr[rows, hs] = o_h

    pl.when(safe)(lambda: run_chunks(False))
    pl.when(jnp.logical_not(safe))(lambda: run_chunks(True))

    if backward:
        o = of_ref[...] + o_scr[...]
        gate = _silu(g_ref[...].astype(F32))
        og = og_ref[...]
        for h in range(A_HEADS):
            hs = slice(h * A_DIM, (h + 1) * A_DIM)
            o_h = o[:, hs]
            y = o_h * lax.rsqrt(jnp.mean(o_h * o_h, axis=-1, keepdims=True) + EPS) * og[:, hs]
            out_ref[:, hs] = (y * gate[:, hs]).astype(out_ref.dtype)


def _hgrn_scratch(rows, backward):
    s = [pltpu.VMEM((A_HEADS, A_DIM, A_DIM), F32),
         pltpu.VMEM((A_CHUNK, A_WIDTH), F32),
         pltpu.VMEM((A_CHUNK, A_WIDTH), F32)]
    if backward:
        s.append(pltpu.VMEM((rows, A_WIDTH), F32))
    return s


def _hgrn_forward(p_qv, p_z, lb_fwd):
    r = p_qv.shape[0]
    rb = ROW_BLOCK
    return pl.pallas_call(
        functools.partial(_hgrn_kernel, backward=False, chunks=rb // A_CHUNK),
        out_shape=jax.ShapeDtypeStruct((r, A_WIDTH), F32),
        grid=(r // rb,),
        in_specs=[pl.BlockSpec((rb, A_WIDTH), lambda i: (i, 0)),
                  pl.BlockSpec((rb, A_WIDTH), lambda i: (i, 1)),
                  pl.BlockSpec((rb, A_WIDTH), lambda i: (i, 0)),
                  pl.BlockSpec((1, A_WIDTH), lambda i: (0, 0))],
        out_specs=pl.BlockSpec((rb, A_WIDTH), lambda i: (i, 0)),
        scratch_shapes=_hgrn_scratch(rb, False),
        compiler_params=_cparams(("arbitrary",)),
        name="hgrn_fwd",
    )(p_qv, p_qv, p_z, lb_fwd.reshape(1, A_WIDTH))


def _hgrn_backward(p_qv, p_z, p_rest, o_f, lb_bwd, onorm_g, n_ctx_blocks):
    r = p_qv.shape[0]
    rb = ROW_BLOCK
    nblk = r // rb

    def blk(s):
        return jnp.where(s < n_ctx_blocks, n_ctx_blocks - 1 - s, nblk - 1 - (s - n_ctx_blocks))

    g_col = REST_G // A_WIDTH
    return pl.pallas_call(
        functools.partial(_hgrn_kernel, backward=True, chunks=rb // A_CHUNK),
        out_shape=jax.ShapeDtypeStruct((r, A_WIDTH), BF),
        grid=(nblk,),
        in_specs=[pl.BlockSpec((rb, A_WIDTH), lambda s: (blk(s), 0)),
                  pl.BlockSpec((rb, A_WIDTH), lambda s: (blk(s), 1)),
                  pl.BlockSpec((rb, A_WIDTH), lambda s: (blk(s), 1)),
                  pl.BlockSpec((1, A_WIDTH), lambda s: (0, 0)),
                  pl.BlockSpec((rb, A_WIDTH), lambda s: (blk(s), 0)),
                  pl.BlockSpec((rb, A_WIDTH), lambda s: (blk(s), g_col)),
                  pl.BlockSpec((1, A_WIDTH), lambda s: (0, 0))],
        out_specs=pl.BlockSpec((rb, A_WIDTH), lambda s: (blk(s), 0)),
        scratch_shapes=_hgrn_scratch(rb, True),
        compiler_params=_cparams(("arbitrary",)),
        name="hgrn_bwd",
    )(p_qv, p_qv, p_z, lb_bwd.reshape(1, A_WIDTH), o_f, p_rest, onorm_g.reshape(1, A_WIDTH))


def _conv_kernel(ap_ref, ac_ref, an_ref, gp_ref, gc_ref, gn_ref, w_ref, b_ref, lg_ref, lbias_ref,
                 o_ref, hext, acc_scr, *, nblk, n_ctx_blocks):
    i = pl.program_id(0)
    rb = ROW_BLOCK

    def glu(a_ref, g_ref):
        return a_ref[...].astype(F32) * jax.nn.sigmoid(g_ref[...].astype(F32))

    prev_ok = jnp.logical_and(i != 0, i != n_ctx_blocks)
    next_ok = jnp.logical_and(i != n_ctx_blocks - 1, i != nblk - 1)
    hext[0:B_HALO] = jnp.where(prev_ok, glu(ap_ref, gp_ref), 0.0)
    hext[B_HALO:B_HALO + rb] = glu(ac_ref, gc_ref)
    hext[B_HALO + rb:B_HALO + rb + B_HALO] = jnp.where(next_ok, glu(an_ref, gn_ref), 0.0)

    row0 = B_HALO - B_KSIZE // 2
    sub = 8
    rt, ct = 128, 128
    for cc in range(B_WIDTH // ct):
        cs = slice(cc * ct, (cc + 1) * ct)
        for rc in range(rb // rt):
            t0 = rc * rt
            acc = jnp.zeros((rt, ct), F32) + b_ref[:, cs]
            for s in range(sub):
                part = None
                for j in range(B_KSIZE):
                    if (row0 + j) % sub != s:
                        continue
                    start = t0 + row0 + j - s
                    term = w_ref[j:j + 1, cs] * hext[start:start + rt + sub, cs]
                    part = term if part is None else part + term
                if part is not None:
                    acc = acc + part[s:s + rt]
            acc_scr[t0:t0 + rt, cs] = acc

    h = acc_scr[...]
    hc = h - jnp.mean(h, axis=-1, keepdims=True)
    y = hc * lax.rsqrt(jnp.mean(hc * hc, axis=-1, keepdims=True) + EPS) * lg_ref[...] + lbias_ref[...]
    o_ref[...] = _silu(y).astype(o_ref.dtype)


def _conv_module(p_rest, conv_w, conv_b, ln_g, ln_b, n_ctx_blocks):
    r = p_rest.shape[0]
    rb = ROW_BLOCK
    nblk = r // rb
    hpb = rb // B_HALO
    nh = r // B_HALO
    ca, cg = REST_CONV_A // B_WIDTH, REST_CONV_G // B_WIDTH

    def prev(i):
        return jnp.maximum(i * hpb - 1, 0)

    def nxt(i):
        return jnp.minimum((i + 1) * hpb, nh - 1)

    vec = lambda: pl.BlockSpec((1, B_WIDTH), lambda i: (0, 0))
    return pl.pallas_call(
        functools.partial(_conv_kernel, nblk=nblk, n_ctx_blocks=n_ctx_blocks),
        out_shape=jax.ShapeDtypeStruct((r, B_WIDTH), BF),
        grid=(nblk,),
        in_specs=[pl.BlockSpec((B_HALO, B_WIDTH), lambda i: (prev(i), ca)),
                  pl.BlockSpec((rb, B_WIDTH), lambda i: (i, ca)),
                  pl.BlockSpec((B_HALO, B_WIDTH), lambda i: (nxt(i), ca)),
                  pl.BlockSpec((B_HALO, B_WIDTH), lambda i: (prev(i), cg)),
                  pl.BlockSpec((rb, B_WIDTH), lambda i: (i, cg)),
                  pl.BlockSpec((B_HALO, B_WIDTH), lambda i: (nxt(i), cg)),
                  pl.BlockSpec((B_KSIZE, B_WIDTH), lambda i: (0, 0)),
                  vec(), vec(), vec()],
        out_specs=pl.BlockSpec((rb, B_WIDTH), lambda i: (i, 0)),
        scratch_shapes=[pltpu.VMEM((rb + 2 * B_HALO, B_WIDTH), F32), pltpu.VMEM((rb, B_WIDTH), F32)],
        compiler_params=_cparams(("arbitrary",)),
        name="conv_module",
    )(p_rest, p_rest, p_rest, p_rest, p_rest, p_rest, conv_w,
      conv_b.reshape(1, B_WIDTH), ln_g.reshape(1, B_WIDTH), ln_b.reshape(1, B_WIDTH))


def _rope(x, cos, sin):
    lane = lax.broadcasted_iota(jnp.int32, x.shape, 1)
    quarter = C_HDIM // 4
    swapped = jnp.where((lane & (2 * quarter - 1)) < quarter,
                        pltpu.roll(x, C_HDIM - quarter, 1), pltpu.roll(x, quarter, 1))
    return x * cos + swapped * sin


def _attn_kernel(sink_ref, q_ref, kp_ref, kc_ref, kn_ref, vp_ref, vc_ref, vn_ref, ck_ref, cv_ref,
                 cq_ref, sq_ref, cp_ref, sp_ref, cc_ref, sc_ref, cn_ref, sn_ref, o_ref,
                 *, n_ctx_blocks, seq, n_ctx):
    i = pl.program_id(0)
    blk = C_BLOCK
    nloc = 3 * blk
    nq = C_GROUP * blk
    scale = C_HDIM ** -0.5

    row = lax.broadcasted_iota(jnp.int32, (nq, nloc), 0)
    col = lax.broadcasted_iota(jnp.int32, (nq, nloc), 1)
    t = row & (blk - 1)
    kpos = (i - n_ctx_blocks - 1) * blk + col
    valid = (col >= t) & (col <= t + 2 * blk) & (kpos >= 0) & (kpos < seq) & (i >= n_ctx_blocks)
    rgrp = lax.broadcasted_iota(jnp.int32, (nq, 1), 0) // blk

    cq, sq = cq_ref[...], sq_ref[...]
    for g in range(C_KVHEADS):
        ks = slice(g * C_HDIM, (g + 1) * C_HDIM)
        kparts = [_rope(kp_ref[:, ks].astype(F32), cp_ref[...], sp_ref[...]).astype(BF),
                  _rope(kc_ref[:, ks].astype(F32), cc_ref[...], sc_ref[...]).astype(BF),
                  _rope(kn_ref[:, ks].astype(F32), cn_ref[...], sn_ref[...]).astype(BF),
                  ck_ref[:, ks]]
        k_all = jnp.concatenate(kparts, axis=0)
        v_all = jnp.concatenate([vp_ref[:, ks], vc_ref[:, ks], vn_ref[:, ks], cv_ref[:, ks]], axis=0)
        qparts, sink_col = [], jnp.zeros((nq, 1), F32)
        for j in range(C_GROUP):
            h = g * C_GROUP + j
            qparts.append(_rope(q_ref[:, h * C_HDIM:(h + 1) * C_HDIM].astype(F32), cq, sq).astype(BF))
            sink_col = jnp.where(rgrp == j, sink_ref[h], sink_col)
        q_all = jnp.concatenate(qparts, axis=0)
        s = lax.dot_general(q_all, k_all, (((1,), (1,)), ((), ())), preferred_element_type=F32) * scale
        s_loc = jnp.where(valid, s[:, :nloc], -1e30)
        s_ctx = s[:, nloc:]
        m = jnp.maximum(jnp.maximum(jnp.max(s_loc, axis=-1, keepdims=True),
                                    jnp.max(s_ctx, axis=-1, keepdims=True)), sink_col)
        p_loc = jnp.exp(s_loc - m)
        p_ctx = jnp.exp(s_ctx - m)
        den = (jnp.sum(p_loc, axis=-1, keepdims=True) + jnp.sum(p_ctx, axis=-1, keepdims=True)
               + jnp.exp(sink_col - m))
        o = jnp.dot(p_loc.astype(BF), v_all[:nloc], preferred_element_type=F32)
        o = o + jnp.dot(p_ctx.astype(BF), v_all[nloc:], preferred_element_type=F32)
        o = o / den
        for j in range(C_GROUP):
            h = g * C_GROUP + j
            o_ref[:, h * C_HDIM:(h + 1) * C_HDIM] = o[j * blk:(j + 1) * blk].astype(o_ref.dtype)


def _window_attention(p_rest, rope_cos, rope_sin, sink, n_ctx, seq):
    r = p_rest.shape[0]
    blk = C_BLOCK
    nblk = r // blk
    ncb = n_ctx // blk
    qw = C_QHEADS * C_HDIM
    kvw = C_KVHEADS * C_HDIM
    qc, kc, vc = REST_CQ // qw, REST_CK // kvw, REST_CV // kvw

    def prev(i):
        return jnp.maximum(i - 1, 0)

    def nxt(i):
        return jnp.minimum(i + 1, nblk - 1)

    def kv(col, f):
        return pl.BlockSpec((blk, kvw), lambda i: (f(i), col))

    def tab(f):
        return pl.BlockSpec((blk, C_HDIM), lambda i: (f(i), 0))

    same = lambda i: i
    return pl.pallas_call(
        functools.partial(_attn_kernel, n_ctx_blocks=ncb, seq=seq, n_ctx=n_ctx),
        out_shape=jax.ShapeDtypeStruct((r, qw), BF),
        grid=(nblk,),
        in_specs=[pl.BlockSpec(memory_space=pltpu.SMEM),
                  pl.BlockSpec((blk, qw), lambda i: (i, qc)),
                  kv(kc, prev), kv(kc, same), kv(kc, nxt),
                  kv(vc, prev), kv(vc, same), kv(vc, nxt),
                  pl.BlockSpec((n_ctx, kvw), lambda i: (0, kc)),
                  pl.BlockSpec((n_ctx, kvw), lambda i: (0, vc)),
                  tab(same), tab(same), tab(prev), tab(prev), tab(same), tab(same), tab(nxt), tab(nxt)],
        out_specs=pl.BlockSpec((blk, qw), lambda i: (i, 0)),
        compiler_params=_cparams(("arbitrary",)),
        name="window_attn",
    )(sink, p_rest, p_rest, p_rest, p_rest, p_rest, p_rest, p_rest, p_rest, p_rest,
      rope_cos, rope_sin, rope_cos, rope_sin, rope_cos, rope_sin, rope_cos, rope_sin)


def _rope_tables(seq, n_ctx):
    pos = jnp.arange(seq)
    rows = (pos // GRID_W).astype(F32)
    cols = (pos % GRID_W).astype(F32)
    half = C_HDIM // 2
    inv = ROPE_BASE ** (-jnp.arange(0, half, 2, dtype=F32) / half)
    ar, ac = rows[:, None] * inv, cols[:, None] * inv
    cos = jnp.concatenate([jnp.cos(ar), jnp.cos(ar), jnp.cos(ac), jnp.cos(ac)], axis=-1)
    sin = jnp.concatenate([-jnp.sin(ar), jnp.sin(ar), -jnp.sin(ac), jnp.sin(ac)], axis=-1)
    cos = jnp.concatenate([jnp.ones((n_ctx, C_HDIM), F32), cos], axis=0)
    sin = jnp.concatenate([jnp.zeros((n_ctx, C_HDIM), F32), sin], axis=0)
    return cos, sin


def _merge_kernel(ya_ref, yb_ref, yc_ref, ga_ref, gb_ref, gc_ref, wa_ref, wb_ref, wc_ref, o_ref,
                  wab_ref, wbb_ref, wcb_ref):
    @pl.when(pl.program_id(1) == 0)
    def _():
        wab_ref[...] = wa_ref[0].astype(BF)
        wbb_ref[...] = wb_ref[0].astype(BF)
        wcb_ref[...] = wc_ref[0].astype(BF)

    def branch(y_ref, g_ref, w_ref):
        return jax.nn.sigmoid(g_ref[...].astype(F32)) * jnp.dot(y_ref[...], w_ref[...],
                                                                preferred_element_type=F32)
    acc = branch(ya_ref, ga_ref, wab_ref) + branch(yb_ref, gb_ref, wbb_ref) + branch(yc_ref, gc_ref, wcb_ref)
    o_ref[...] = acc.astype(o_ref.dtype)


def _merge(ya, yb, yc, p_rest, w_a, w_b, w_c, layer):
    r = ya.shape[0]
    tm, tn = _row_tile(r), 512
    gb0 = REST_GATES // tn
    nper = D // tn

    def y_spec():
        return pl.BlockSpec((tm, ya.shape[1]), lambda n, i: (i, 0))

    def g_spec(br):
        return pl.BlockSpec((tm, tn), lambda n, i: (i, gb0 + br * nper + n))

    def w_spec():
        return pl.BlockSpec((1, ya.shape[1], tn), lambda n, i: (layer, 0, n))

    return pl.pallas_call(
        _merge_kernel,
        out_shape=jax.ShapeDtypeStruct((r, D), BF),
        grid=(D // tn, r // tm),
        in_specs=[y_spec(), y_spec(), y_spec(), g_spec(0), g_spec(1), g_spec(2), w_spec(), w_spec(), w_spec()],
        out_specs=pl.BlockSpec((tm, tn), lambda n, i: (i, n)),
        scratch_shapes=[pltpu.VMEM((ya.shape[1], tn), BF) for _ in range(3)],
        compiler_params=_cparams(("arbitrary", "arbitrary")),
        name="branch_merge",
    )(ya, yb, yc, p_rest, p_rest, p_rest, w_a, w_b, w_c)


def _out_proj_kernel(m_ref, w_ref, x_ref, mod_ref, g_ref, xo_ref, h_ref, *, n_ctx_blocks):
    is_ctx = pl.program_id(0) < n_ctx_blocks
    y = jnp.dot(m_ref[...], w_ref[0], preferred_element_type=F32)
    x = x_ref[...] + _mod_row(mod_ref, is_ctx, 2) * y
    xo_ref[...] = x
    h = x * lax.rsqrt(jnp.mean(x * x, axis=-1, keepdims=True) + EPS) * g_ref[...]
    h_ref[...] = h * (1.0 + _mod_row(mod_ref, is_ctx, 4)) + _mod_row(mod_ref, is_ctx, 3)


def _out_proj(merged, w_out, x, mod, g2, layer, n_ctx_blocks):
    r = x.shape[0]
    rb = ROW_BLOCK
    return pl.pallas_call(
        functools.partial(_out_proj_kernel, n_ctx_blocks=n_ctx_blocks),
        out_shape=(jax.ShapeDtypeStruct((r, D), F32), jax.ShapeDtypeStruct((r, D), F32)),
        grid=(r // rb,),
        in_specs=[pl.BlockSpec((rb, D), lambda i: (i, 0)),
                  pl.BlockSpec((1, D, D), lambda i: (layer, 0, 0)),
                  pl.BlockSpec((rb, D), lambda i: (i, 0)),
                  pl.BlockSpec((8, 6 * D), lambda i: (0, 0)),
                  pl.BlockSpec((1, D), lambda i: (0, 0))],
        out_specs=(pl.BlockSpec((rb, D), lambda i: (i, 0)), pl.BlockSpec((rb, D), lambda i: (i, 0))),
        compiler_params=_cparams(("arbitrary",)),
        name="out_proj",
    )(merged, w_out, x, mod, g2.reshape(1, D))


def _router_kernel(h_ref, w_ref, o_ref):
    logits = lax.dot_general(w_ref[...], h_ref[...], (((1,), (1,)), ((), ())),
                             precision=HI, preferred_element_type=F32)
    e = jnp.exp(logits - jnp.max(logits, axis=0, keepdims=True))
    o_ref[0] = e / jnp.sum(e, axis=0, keepdims=True)


def _router(h2, w_router_t):
    r = h2.shape[0]
    nblk = r // LANES
    return pl.pallas_call(
        _router_kernel,
        out_shape=jax.ShapeDtypeStruct((nblk, N_EXPERTS, LANES), F32),
        grid=(nblk,),
        in_specs=[pl.BlockSpec((LANES, D), lambda i: (i, 0)),
                  pl.BlockSpec((N_EXPERTS, D), lambda i: (0, 0))],
        out_specs=pl.BlockSpec((1, N_EXPERTS, LANES), lambda i: (i, 0, 0)),
        compiler_params=_cparams(("arbitrary",)),
        name="router",
    )(h2, w_router_t)


def _cumsum_tokens(m):
    nb = m.shape[0]
    m2 = m.reshape(nb * N_EXPERTS, LANES).astype(BF)
    li = lax.broadcasted_iota(jnp.int32, (LANES, LANES), 0)
    lj = lax.broadcasted_iota(jnp.int32, (LANES, LANES), 1)
    tri = jnp.where(li <= lj, 1.0, 0.0).astype(BF)
    ones = jnp.ones((LANES, LANES), BF)
    loc = jnp.dot(m2, tri, preferred_element_type=F32).reshape(nb, N_EXPERTS, LANES)
    tot = jnp.dot(m2, ones, preferred_element_type=F32).reshape(nb, N_EXPERTS, LANES)
    offs, run = [], jnp.zeros((N_EXPERTS, LANES), F32)
    for b in range(nb):
        offs.append(run)
        run = run + tot[b]
    before = jnp.stack(offs, axis=0)
    return loc + before, before


def _select_kernel(aff_ref, slot_ref, offs_ref, *, sets):
    for b0, nb, cap in sets:
        aff = aff_ref[b0:b0 + nb]

        def count(mask):
            c = jnp.sum(jnp.where(mask, 1.0, 0.0), axis=0)
            return jnp.sum(c, axis=1, keepdims=True)

        def as_float(word):
            return pltpu.bitcast(word, F32)

        def search(it, word):
            cand = word | jnp.left_shift(jnp.int32(1), 30 - it)
            return jnp.where(count(aff >= as_float(cand)[None]) >= cap, cand, word)

        thr = as_float(lax.fori_loop(0, 31, search, jnp.zeros((N_EXPERTS, 1), jnp.int32)))[None]
        gt = aff > thr
        eq = aff == thr
        need = (cap - count(gt))[None]
        eq_rank, _ = _cumsum_tokens(jnp.where(eq, 1.0, 0.0))
        sel = gt | (eq & (eq_rank <= need))
        rank, before = _cumsum_tokens(jnp.where(sel, 1.0, 0.0))
        slot_ref[b0:b0 + nb] = jnp.where(sel, rank - 1.0, -1.0)
        offs_ref[b0:b0 + nb] = before


def _select(aff, sets):
    shape = jax.ShapeDtypeStruct(aff.shape, F32)
    return pl.pallas_call(
        functools.partial(_select_kernel, sets=sets),
        out_shape=(shape, shape),
        compiler_params=pltpu.CompilerParams(vmem_limit_bytes=VMEM_LIMIT),
        name="expert_select",
    )(aff)


def _compact_kernel(offs_ref, slot_ref, aff_ref, idx_ref, gate_ref, acc_i, acc_g, *, sets):
    e = pl.program_id(0)
    win = LANES + 8
    lane = lax.broadcasted_iota(jnp.int32, (1, LANES), 1)
    w_iota = lax.broadcasted_iota(jnp.int32, (win, LANES), 0)
    slot0 = 0
    for b0, nb, cap in sets:
        acc_i[...] = jnp.zeros_like(acc_i)
        acc_g[...] = jnp.zeros_like(acc_g)

        def per_block(bi, carry):
            b = b0 + bi
            start = pl.multiple_of((offs_ref[b * N_EXPERTS + e] // 8) * 8, 8)
            hit = (w_iota + start).astype(F32) == slot_ref[b, pl.ds(e, 1), :]
            tok = (b * LANES + lane).astype(F32)
            acc_i[pl.ds(start, win), :] += jnp.where(hit, tok, 0.0)
            acc_g[pl.ds(start, win), :] += jnp.where(hit, aff_ref[b, pl.ds(e, 1), :], 0.0)
            return carry

        lax.fori_loop(0, nb, per_block, 0)
        idx_ref[0, slot0:slot0 + cap, :] = jnp.sum(acc_i[0:cap], axis=1, keepdims=True).astype(jnp.int32)
        gate_ref[0, slot0:slot0 + cap, :] = jnp.sum(acc_g[0:cap], axis=1, keepdims=True)
        slot0 += cap


def _compact(offs, slot, aff, sets):
    slots = sum(cap for _, _, cap in sets)
    max_cap = max(cap for _, _, cap in sets)
    whole = pl.BlockSpec(aff.shape, lambda e, offs: (0, 0, 0))
    out = pl.BlockSpec((1, slots, 1), lambda e, offs: (e, 0, 0))
    grid_spec = pltpu.PrefetchScalarGridSpec(
        num_scalar_prefetch=1,
        grid=(N_EXPERTS,),
        in_specs=[whole, whole],
        out_specs=(out, out),
        scratch_shapes=[pltpu.VMEM((max_cap + LANES + 8, LANES), F32) for _ in range(2)])
    return pl.pallas_call(
        functools.partial(_compact_kernel, sets=sets),
        out_shape=(jax.ShapeDtypeStruct((N_EXPERTS, slots, 1), jnp.int32),
                   jax.ShapeDtypeStruct((N_EXPERTS, slots, 1), F32)),
        grid_spec=grid_spec,
        compiler_params=_cparams(("arbitrary",)),
        name="expert_compact",
    )(offs, slot, aff)


def _routing_sets(n_ctx, seq):
    nb_ctx, nb_lat = n_ctx // LANES, seq // LANES
    return ((0, nb_ctx, EC_CAPACITY * n_ctx // N_EXPERTS), (nb_ctx, nb_lat, EC_CAPACITY * seq // N_EXPERTS))


def _ffn_kernel(idx_ref, h_hbm, x_hbm, gate_ref, mod_ref, wg_ref, wu_ref, wd_ref, xo_hbm,
                xs, xsb, xr, acc, sem_h, sem_x, sem_s, *, slots, ctx_slots, nf, n_experts):
    del x_hbm
    assert nf >= 4
    e = pl.program_id(0)
    f = pl.program_id(1)

    def gather_h(expert, p):
        row = idx_ref[expert * slots + p]
        return pltpu.make_async_copy(h_hbm.at[pl.ds(row, 1)], xs.at[pl.ds(p, 1)], sem_h)

    def all_h():
        return pltpu.make_async_copy(h_hbm.at[pl.ds(0, slots)], xs, sem_h)

    def all_x():
        return pltpu.make_async_copy(xo_hbm.at[pl.ds(0, slots)], xr, sem_x)

    def all_scatter():
        return pltpu.make_async_copy(xr, xo_hbm.at[pl.ds(0, slots)], sem_s)

    def ffn_chunk(first):
        xb = xsb[...]
        hid = _silu(jnp.dot(xb, wg_ref[0].astype(BF), preferred_element_type=F32)) * jnp.dot(
            xb, wu_ref[0].astype(BF), preferred_element_type=F32)
        out = jnp.dot(hid.astype(BF), wd_ref[0].astype(BF), preferred_element_type=F32)
        if first:
            acc[...] = out
        else:
            acc[...] += out

    @pl.when(jnp.logical_and(f == 0, e == 0))
    def _():
        def issue(grp, carry):
            p0 = pl.multiple_of(grp * ISSUE_UNROLL, ISSUE_UNROLL)
            for u in range(ISSUE_UNROLL):
                gather_h(0, p0 + u).start()
            return carry

        lax.fori_loop(0, slots // ISSUE_UNROLL, issue, 0)

    @pl.when(f == 0)
    def _():
        all_h().wait()
        xsb[...] = xs[...].astype(BF)
        ffn_chunk(True)

    @pl.when(jnp.logical_and(f == 1, e > 0))
    def _():
        all_scatter().wait()

    @pl.when(f == 1)
    def _():
        base = e * slots
        for p in range(slots):
            row = idx_ref[base + p]
            pltpu.make_async_copy(xo_hbm.at[pl.ds(row, 1)], xr.at[pl.ds(p, 1)], sem_x).start()
        ffn_chunk(False)

    @pl.when(f == 2)
    def _():
        nxt = jnp.minimum(e + 1, n_experts - 1)
        for p in range(slots):
            gather_h(nxt, p).start()
        ffn_chunk(False)

    @pl.when(jnp.logical_and(f > 2, f < nf - 1))
    def _():
        ffn_chunk(False)

    @pl.when(f == nf - 1)
    def _():
        all_x().wait()
        ffn_chunk(False)
        is_ctx = lax.broadcasted_iota(jnp.int32, (slots, 1), 0) < ctx_slots
        m5 = jnp.where(is_ctx, mod_ref[1:2, 5 * D:6 * D], mod_ref[0:1, 5 * D:6 * D])
        xr[...] = xr[...] + m5 * (acc[...] * gate_ref[0])
        base = e * slots
        for p in range(slots):
            row = idx_ref[base + p]
            pltpu.make_async_copy(xr.at[pl.ds(p, 1)], xo_hbm.at[pl.ds(row, 1)], sem_s).start()

    @pl.when(jnp.logical_and(f == nf - 1, e == n_experts - 1))
    def _():
        all_scatter().wait()
        all_h().wait()


def _expert_ffn(idx, h2, x, gate, mod, w_gate, w_up, w_down, layer, ctx_slots):
    n_exp, slots = idx.shape
    dff = w_gate.shape[-1]
    nf = dff // FFN_FCHUNK
    fc = FFN_FCHUNK
    grid_spec = pltpu.PrefetchScalarGridSpec(
        num_scalar_prefetch=1,
        grid=(n_exp, nf),
        in_specs=[pl.BlockSpec(memory_space=pl.ANY),
                  pl.BlockSpec(memory_space=pl.ANY),
                  pl.BlockSpec((1, slots, 1), lambda e, f, idx: (e, 0, 0)),
                  pl.BlockSpec((8, 6 * D), lambda e, f, idx: (0, 0)),
                  pl.BlockSpec((1, D, fc), lambda e, f, idx: (layer * n_exp + e, 0, f)),
                  pl.BlockSpec((1, D, fc), lambda e, f, idx: (layer * n_exp + e, 0, f)),
                  pl.BlockSpec((1, fc, D), lambda e, f, idx: (layer * n_exp + e, f, 0))],
        out_specs=pl.BlockSpec(memory_space=pl.ANY),
        scratch_shapes=[pltpu.VMEM((slots, D), F32), pltpu.VMEM((slots, D), BF),
                        pltpu.VMEM((slots, D), F32), pltpu.VMEM((slots, D), F32),
                        pltpu.SemaphoreType.DMA, pltpu.SemaphoreType.DMA, pltpu.SemaphoreType.DMA])
    return pl.pallas_call(
        functools.partial(_ffn_kernel, slots=slots, ctx_slots=ctx_slots, nf=nf, n_experts=n_exp),
        out_shape=jax.ShapeDtypeStruct(x.shape, F32),
        grid_spec=grid_spec,
        input_output_aliases={2: 0},
        compiler_params=_cparams(("arbitrary", "arbitrary")),
        name="expert_ffn",
    )(idx.reshape(-1), h2, x, gate, mod, w_gate, w_up, w_down)


def _final_norm_kernel(x_ref, g_ref, o_ref):
    x = x_ref[...]
    o_ref[...] = x * lax.rsqrt(jnp.mean(x * x, axis=-1, keepdims=True) + EPS) * g_ref[...]


def _final_norm(x, g, n_ctx_blocks):
    r = x.shape[0]
    rb = ROW_BLOCK
    nlat = r // rb - n_ctx_blocks
    return pl.pallas_call(
        _final_norm_kernel,
        out_shape=jax.ShapeDtypeStruct((nlat * rb, D), F32),
        grid=(nlat,),
        in_specs=[pl.BlockSpec((rb, D), lambda i: (i + n_ctx_blocks, 0)),
                  pl.BlockSpec((1, D), lambda i: (0, 0))],
        out_specs=pl.BlockSpec((rb, D), lambda i: (i, 0)),
        compiler_params=_cparams(("arbitrary",)),
        name="final_norm",
    )(x, g.reshape(1, D))


def _lower_bounds(lb_logits):
    p = jax.nn.softmax(lb_logits.astype(F32), axis=0)
    cs = jnp.cumsum(p, axis=0)
    return cs - cs[0]


def kernel(x, c, ctx, c_ctx, w_ada, b_ada, norm1_g, norm2_g, w_in, hgrn_lb_logits, hgrn_onorm_g,
           conv_w, conv_b, conv_ln_g, conv_ln_b, attn_sink, w_branch_a, w_branch_b, w_branch_c,
           w_out, w_router, w_exp_gate, w_exp_up, w_exp_down, final_norm_g):
    bsz, seq, _ = x.shape
    n_ctx = ctx.shape[1]
    depth = w_in.shape[0]
    assert bsz == 1 and x.shape[2] == D and w_in.shape[2] == D_IN
    assert n_ctx % ROW_BLOCK == 0 and seq % ROW_BLOCK == 0 and seq % GRID_W == 0
    ncb = n_ctx // ROW_BLOCK
    sets = _routing_sets(n_ctx, seq)
    ctx_slots = sets[0][2]

    xs = jnp.concatenate([ctx[0], x[0]], axis=0)
    c8 = jnp.zeros((8, D), F32).at[0].set(c[0]).at[1].set(c_ctx)
    mod = _modulation(c8, w_ada, b_ada)
    lbs = _lower_bounds(hgrn_lb_logits)
    rope_cos, rope_sin = _rope_tables(seq, n_ctx)

    w_a_b, w_b_b, w_c_b = w_branch_a, w_branch_b, w_branch_c
    w_out_b = w_out.astype(BF)
    n_exp = w_exp_gate.shape[1]
    w_g_b = w_exp_gate.reshape(depth * n_exp, D, -1)
    w_u_b = w_exp_up.reshape(depth * n_exp, D, -1)
    w_d_b = w_exp_down.reshape(depth * n_exp, -1, D)

    for l in range(depth):
        hx = _norm_mod(xs, norm1_g[l], mod[l], 0, 1, ncb, BF)
        p_qv = _matmul_cols(hx, w_in, l, SEG_QV[0], SEG_QV[1] - SEG_QV[0], BF, tn=1024)
        p_z = _matmul_cols(hx, w_in, l, SEG_Z[0], SEG_Z[1] - SEG_Z[0], F32, tn=1024)
        p_rest = _matmul_cols(hx, w_in, l, SEG_REST0, REST_WIDTH, BF, tn=512)

        o_f = _hgrn_forward(p_qv, p_z, lbs[l, 0])
        ya = _hgrn_backward(p_qv, p_z, p_rest, o_f, lbs[l, 1], hgrn_onorm_g[l], ncb)
        yb = _conv_module(p_rest, conv_w[l], conv_b[l], conv_ln_g[l], conv_ln_b[l], ncb)
        yc = _window_attention(p_rest, rope_cos, rope_sin, attn_sink[l], n_ctx, seq)

        merged = _merge(ya, yb, yc, p_rest, w_a_b, w_b_b, w_c_b, l)
        xs, h2 = _out_proj(merged, w_out_b, xs, mod[l], norm2_g[l], l, ncb)

        aff = _router(h2, w_router[l].T)
        slot, before = _select(aff, sets)
        offs = before[:, :, 0].astype(jnp.int32).reshape(-1)
        idx, gate = _compact(offs, slot, aff, sets)
        xs = _expert_ffn(idx.reshape(n_exp, -1), h2, xs, gate, mod[l], w_g_b, w_u_b, w_d_b, l, ctx_slots)

    return _final_norm(xs, final_norm_g, ncb)[None]
```

```python
import functools

import jax
import jax.numpy as jnp
import numpy as np
from jax import lax
from jax.experimental import pallas as pl
from jax.experimental.pallas import tpu as pltpu

F32 = jnp.float32
BF = jnp.bfloat16
HI = lax.Precision.HIGHEST

D = 2048
EPS = 1e-6
GRID_W = 64
ROPE_BASE = 10000.0

A_HEADS = 8
A_DIM = 128
A_WIDTH = A_HEADS * A_DIM
A_CHUNK = 64
A_SAFE_RANGE = 120.0

B_WIDTH = 1024
B_KSIZE = 31
B_HALO = 16

C_QHEADS = 8
C_KVHEADS = 2
C_GROUP = C_QHEADS // C_KVHEADS
C_HDIM = 128
C_BLOCK = 128

N_EXPERTS = 16
EC_CAPACITY = 2
EXPERT_DFF = 1024
FFN_FCHUNK = 256
ISSUE_UNROLL = 8

ROW_BLOCK = 256
LANES = 128
VMEM_LIMIT = 56 * 1024 * 1024

SEG_QV = (0, 2 * A_WIDTH)
SEG_Z = (2 * A_WIDTH, 4 * A_WIDTH)
SEG_REST0 = 4 * A_WIDTH
REST_G = 0
REST_CONV_A = A_WIDTH
REST_CONV_G = A_WIDTH + B_WIDTH
REST_CQ = A_WIDTH + 2 * B_WIDTH
REST_CK = REST_CQ + C_QHEADS * C_HDIM
REST_CV = REST_CK + C_KVHEADS * C_HDIM
REST_GATES = REST_CV + C_KVHEADS * C_HDIM
REST_WIDTH = REST_GATES + 3 * D
D_IN = SEG_REST0 + REST_WIDTH


def _cparams(sem, vmem=VMEM_LIMIT):
    return pltpu.CompilerParams(dimension_semantics=sem, vmem_limit_bytes=vmem)


def _silu(x):
    return x * jax.nn.sigmoid(x)


def _mod_kernel(c_ref, w_ref, b_ref, o_ref):
    s = _silu(c_ref[...]).astype(BF)
    o_ref[0] = jnp.dot(s, w_ref[0].astype(BF), preferred_element_type=F32) + b_ref[0]


def _modulation(c8, w_ada, b_ada):
    depth, _, n6 = w_ada.shape
    tn = D // 2
    return pl.pallas_call(
        _mod_kernel,
        out_shape=jax.ShapeDtypeStruct((depth, 8, n6), F32),
        grid=(depth, n6 // tn),
        in_specs=[pl.BlockSpec((8, D), lambda l, n: (0, 0)),
                  pl.BlockSpec((1, D, tn), lambda l, n: (l, 0, n)),
                  pl.BlockSpec((1, 1, tn), lambda l, n: (l, 0, n))],
        out_specs=pl.BlockSpec((1, 8, tn), lambda l, n: (l, 0, n)),
        compiler_params=_cparams(("arbitrary", "arbitrary")),
        name="adaln_mod",
    )(c8, w_ada, b_ada.reshape(depth, 1, n6))


def _mod_row(mod_ref, is_ctx, i):
    row = jnp.where(is_ctx, 1, 0)
    return mod_ref[pl.ds(row, 1), i * D:(i + 1) * D]


def _norm_mod_kernel(x_ref, g_ref, mod_ref, o_ref, *, shift_i, scale_i, n_ctx_blocks):
    x = x_ref[...]
    y = x * lax.rsqrt(jnp.mean(x * x, axis=-1, keepdims=True) + EPS) * g_ref[...]
    is_ctx = pl.program_id(0) < n_ctx_blocks
    y = y * (1.0 + _mod_row(mod_ref, is_ctx, scale_i)) + _mod_row(mod_ref, is_ctx, shift_i)
    o_ref[...] = y.astype(o_ref.dtype)


def _norm_mod(x, g, mod, shift_i, scale_i, n_ctx_blocks, out_dtype):
    r = x.shape[0]
    return pl.pallas_call(
        functools.partial(_norm_mod_kernel, shift_i=shift_i, scale_i=scale_i, n_ctx_blocks=n_ctx_blocks),
        out_shape=jax.ShapeDtypeStruct((r, D), out_dtype),
        grid=(r // ROW_BLOCK,),
        in_specs=[pl.BlockSpec((ROW_BLOCK, D), lambda i: (i, 0)),
                  pl.BlockSpec((1, D), lambda i: (0, 0)),
                  pl.BlockSpec((8, 6 * D), lambda i: (0, 0))],
        out_specs=pl.BlockSpec((ROW_BLOCK, D), lambda i: (i, 0)),
        compiler_params=_cparams(("arbitrary",)),
        name="norm_mod",
    )(x, g.reshape(1, D), mod)


def _mm_kernel(a_ref, b_ref, o_ref, bb_ref):
    @pl.when(pl.program_id(1) == 0)
    def _():
        bb_ref[...] = b_ref[0].astype(BF)

    o_ref[...] = jnp.dot(a_ref[...], bb_ref[...], preferred_element_type=F32).astype(o_ref.dtype)


def _row_tile(r, candidates=(1408, 768, ROW_BLOCK)):
    return next(t for t in candidates if r % t == 0)


def _matmul_cols(a, w3, layer, col0, ncols, out_dtype, tn):
    m, k = a.shape
    tm = _row_tile(m)
    assert col0 % tn == 0 and ncols % tn == 0
    cb0 = col0 // tn
    return pl.pallas_call(
        _mm_kernel,
        out_shape=jax.ShapeDtypeStruct((m, ncols), out_dtype),
        grid=(ncols // tn, m // tm),
        in_specs=[pl.BlockSpec((tm, k), lambda n, i: (i, 0)),
                  pl.BlockSpec((1, k, tn), lambda n, i: (layer, 0, cb0 + n))],
        out_specs=pl.BlockSpec((tm, tn), lambda n, i: (i, n)),
        scratch_shapes=[pltpu.VMEM((k, tn), BF)],
        compiler_params=_cparams(("arbitrary", "arbitrary")),
        name="in_proj",
    )(a, w3)


def _hgrn_kernel(*refs, backward, chunks):
    if backward:
        (q_ref, v_ref, z_ref, lb_ref, of_ref, g_ref, og_ref, out_ref,
         st_ref, b_scr, k_scr, o_scr) = refs
    else:
        q_ref, v_ref, z_ref, lb_ref, out_ref, st_ref, b_scr, k_scr = refs
        o_scr = out_ref
    cs = A_CHUNK

    @pl.when(pl.program_id(0) == 0)
    def _():
        st_ref[...] = jnp.zeros_like(st_ref)

    lb = lb_ref[...]
    z = z_ref[...]
    log_lb = jnp.log(lb)
    log_sig = jnp.minimum(z, 0.0) - jnp.log(1.0 + jnp.exp(-jnp.abs(z)))
    c = jnp.log(1.0 - lb) + log_sig
    lf = jnp.maximum(log_lb, c) + jnp.log(1.0 + jnp.exp(-jnp.abs(log_lb - c)))
    kk = (1.0 - lb) / (1.0 + jnp.exp(z))

    lo = None
    for ci in range(chunks):
        t = jnp.sum(lf[ci * cs:(ci + 1) * cs], axis=0, keepdims=True)
        lo = t if lo is None else jnp.minimum(lo, t)
    safe = jnp.min(lo) >= -A_SAFE_RANGE

    ri = lax.broadcasted_iota(jnp.int32, (cs, cs), 0)
    cj = lax.broadcasted_iota(jnp.int32, (cs, cs), 1)
    causal = (cj >= ri) if backward else (cj <= ri)
    tri = jnp.where(causal, 1.0, 0.0).astype(BF)
    lane_cs = lax.broadcasted_iota(jnp.int32, (1, cs), 1)
    hi = lax.broadcasted_iota(jnp.int32, (A_WIDTH, LANES), 0) // A_DIM
    hj = lax.broadcasted_iota(jnp.int32, (A_WIDTH, LANES), 1)
    head_ind = jnp.where(hi == hj, 1.0, 0.0).astype(F32)

    order = range(chunks - 1, -1, -1) if backward else range(chunks)

    def run_chunks(pairwise):
        for ci in order:
            rows = slice(ci * cs, (ci + 1) * cs)
            q_c = q_ref[rows, :].astype(F32)
            v_c = v_ref[rows, :]
            k_c = kk[rows]
            lf_c = lf[rows]
            p1 = lf_c.astype(BF)
            r1 = lf_c - p1.astype(F32)
            p2 = r1.astype(BF)
            p3 = (r1 - p2.astype(F32)).astype(BF)
            b = (jnp.dot(tri, p1, preferred_element_type=F32) + jnp.dot(tri, p2, preferred_element_type=F32)
                 + jnp.dot(tri, p3, preferred_element_type=F32))
            b_lo, b_hi = b[0:1], b[cs - 1:cs]
            tot = b_lo if backward else b_hi

            if pairwise:
                b_scr[...] = b
                k_scr[...] = k_c

                def body(s, acc, b=b, q_c=q_c):
                    rb = b_scr[pl.ds(s, 1), :]
                    rk = k_scr[pl.ds(s, 1), :]
                    p = q_c * jnp.exp(jnp.minimum(b - rb, 0.0)) * rk
                    col = jnp.dot(p, head_ind, precision=HI, preferred_element_type=F32)
                    onehot = jnp.where(lane_cs == s, 1.0, 0.0)
                    return tuple(acc[h] + col[:, h:h + 1] * onehot for h in range(A_HEADS))

                intra = lax.fori_loop(0, cs, body, tuple(jnp.zeros((cs, cs), F32) for _ in range(A_HEADS)))
            else:
                r = 0.5 * (b_lo + b_hi)
                qa = (q_c * jnp.exp(b - r)).astype(BF)
                kb = (k_c * jnp.exp(r - b)).astype(BF)
                intra = [lax.dot_general(qa[:, h * A_DIM:(h + 1) * A_DIM], kb[:, h * A_DIM:(h + 1) * A_DIM],
                                         (((1,), (1,)), ((), ())), preferred_element_type=F32)
                         for h in range(A_HEADS)]

            qe = (q_c * jnp.exp(b)).astype(BF)
            ke = (k_c * jnp.exp(tot - b)).astype(BF)
            et = jnp.exp(tot)
            for h in range(A_HEADS):
                hs = slice(h * A_DIM, (h + 1) * A_DIM)
                am = jnp.where(causal, intra[h], 0.0).astype(BF)
                st = st_ref[h]
                o_h = jnp.dot(am, v_c[:, hs], preferred_element_type=F32)
                o_h = o_h + lax.dot_general(qe[:, hs], st.astype(BF), (((1,), (1,)), ((), ())),
                                            preferred_element_type=F32)
                st_ref[h] = st * et[:, hs] + lax.dot_general(v_c[:, hs], ke[:, hs], (((0,), (0,)), ((), ())),
                                                             preferred_element_type=F32)
                o_scr[rows, hs] = o_h

    pl.when(safe)(lambda: run_chunks(False))
    pl.when(jnp.logical_not(safe))(lambda: run_chunks(True))

    if backward:
        o = of_ref[...] + o_scr[...]
        gate = _silu(g_ref[...].astype(F32))
        og = og_ref[...]
        for h in range(A_HEADS):
            hs = slice(h * A_DIM, (h + 1) * A_DIM)
            o_h = o[:, hs]
            y = o_h * lax.rsqrt(jnp.mean(o_h * o_h, axis=-1, keepdims=True) + EPS) * og[:, hs]
            out_ref[:, hs] = (y * gate[:, hs]).astype(out_ref.dtype)


def _hgrn_scratch(rows, backward):
    s = [pltpu.VMEM((A_HEADS, A_DIM, A_DIM), F32),
         pltpu.VMEM((A_CHUNK, A_WIDTH), F32),
         pltpu.VMEM((A_CHUNK, A_WIDTH), F32)]
    if backward:
        s.append(pltpu.VMEM((rows, A_WIDTH), F32))
    return s


def _hgrn_forward(p_qv, p_z, lb_fwd):
    r = p_qv.shape[0]
    rb = ROW_BLOCK
    return pl.pallas_call(
        functools.partial(_hgrn_kernel, backward=False, chunks=rb // A_CHUNK),
        out_shape=jax.ShapeDtypeStruct((r, A_WIDTH), F32),
        grid=(r // rb,),
        in_specs=[pl.BlockSpec((rb, A_WIDTH), lambda i: (i, 0)),
                  pl.BlockSpec((rb, A_WIDTH), lambda i: (i, 1)),
                  pl.BlockSpec((rb, A_WIDTH), lambda i: (i, 0)),
                  pl.BlockSpec((1, A_WIDTH), lambda i: (0, 0))],
        out_specs=pl.BlockSpec((rb, A_WIDTH), lambda i: (i, 0)),
        scratch_shapes=_hgrn_scratch(rb, False),
        compiler_params=_cparams(("arbitrary",)),
        name="hgrn_fwd",
    )(p_qv, p_qv, p_z, lb_fwd.reshape(1, A_WIDTH))


def _hgrn_backward(p_qv, p_z, p_rest, o_f, lb_bwd, onorm_g, n_ctx_blocks):
    r = p_qv.shape[0]
    rb = ROW_BLOCK
    nblk = r // rb

    def blk(s):
        return jnp.where(s < n_ctx_blocks, n_ctx_blocks - 1 - s, nblk - 1 - (s - n_ctx_blocks))

    g_col = REST_G // A_WIDTH
    return pl.pallas_call(
        functools.partial(_hgrn_kernel, backward=True, chunks=rb // A_CHUNK),
        out_shape=jax.ShapeDtypeStruct((r, A_WIDTH), BF),
        grid=(nblk,),
        in_specs=[pl.BlockSpec((rb, A_WIDTH), lambda s: (blk(s), 0)),
                  pl.BlockSpec((rb, A_WIDTH), lambda s: (blk(s), 1)),
                  pl.BlockSpec((rb, A_WIDTH), lambda s: (blk(s), 1)),
                  pl.BlockSpec((1, A_WIDTH), lambda s: (0, 0)),
                  pl.BlockSpec((rb, A_WIDTH), lambda s: (blk(s), 0)),
                  pl.BlockSpec((rb, A_WIDTH), lambda s: (blk(s), g_col)),
                  pl.BlockSpec((1, A_WIDTH), lambda s: (0, 0))],
        out_specs=pl.BlockSpec((rb, A_WIDTH), lambda s: (blk(s), 0)),
        scratch_shapes=_hgrn_scratch(rb, True),
        compiler_params=_cparams(("arbitrary",)),
        name="hgrn_bwd",
    )(p_qv, p_qv, p_z, lb_bwd.reshape(1, A_WIDTH), o_f, p_rest, onorm_g.reshape(1, A_WIDTH))


def _conv_kernel(ap_ref, ac_ref, an_ref, gp_ref, gc_ref, gn_ref, w_ref, b_ref, lg_ref, lbias_ref,
                 o_ref, hext, acc_scr, *, nblk, n_ctx_blocks):
    i = pl.program_id(0)
    rb = ROW_BLOCK

    def glu(a_ref, g_ref):
        return a_ref[...].astype(F32) * jax.nn.sigmoid(g_ref[...].astype(F32))

    prev_ok = jnp.logical_and(i != 0, i != n_ctx_blocks)
    next_ok = jnp.logical_and(i != n_ctx_blocks - 1, i != nblk - 1)
    hext[0:B_HALO] = jnp.where(prev_ok, glu(ap_ref, gp_ref), 0.0)
    hext[B_HALO:B_HALO + rb] = glu(ac_ref, gc_ref)
    hext[B_HALO + rb:B_HALO + rb + B_HALO] = jnp.where(next_ok, glu(an_ref, gn_ref), 0.0)

    row0 = B_HALO - B_KSIZE // 2
    sub = 8
    rt, ct = 128, 128
    for cc in range(B_WIDTH // ct):
        cs = slice(cc * ct, (cc + 1) * ct)
        for rc in range(rb // rt):
            t0 = rc * rt
            acc = jnp.zeros((rt, ct), F32) + b_ref[:, cs]
            for s in range(sub):
                part = None
                for j in range(B_KSIZE):
                    if (row0 + j) % sub != s:
                        continue
                    start = t0 + row0 + j - s
                    term = w_ref[j:j + 1, cs] * hext[start:start + rt + sub, cs]
                    part = term if part is None else part + term
                if part is not None:
                    acc = acc + part[s:s + rt]
            acc_scr[t0:t0 + rt, cs] = acc

    h = acc_scr[...]
    hc = h - jnp.mean(h, axis=-1, keepdims=True)
    y = hc * lax.rsqrt(jnp.mean(hc * hc, axis=-1, keepdims=True) + EPS) * lg_ref[...] + lbias_ref[...]
    o_ref[...] = _silu(y).astype(o_ref.dtype)


def _conv_module(p_rest, conv_w, conv_b, ln_g, ln_b, n_ctx_blocks):
    r = p_rest.shape[0]
    rb = ROW_BLOCK
    nblk = r // rb
    hpb = rb // B_HALO
    nh = r // B_HALO
    ca, cg = REST_CONV_A // B_WIDTH, REST_CONV_G // B_WIDTH

    def prev(i):
        return jnp.maximum(i * hpb - 1, 0)

    def nxt(i):
        return jnp.minimum((i + 1) * hpb, nh - 1)

    vec = lambda: pl.BlockSpec((1, B_WIDTH), lambda i: (0, 0))
    return pl.pallas_call(
        functools.partial(_conv_kernel, nblk=nblk, n_ctx_blocks=n_ctx_blocks),
        out_shape=jax.ShapeDtypeStruct((r, B_WIDTH), BF),
        grid=(nblk,),
        in_specs=[pl.BlockSpec((B_HALO, B_WIDTH), lambda i: (prev(i), ca)),
                  pl.BlockSpec((rb, B_WIDTH), lambda i: (i, ca)),
                  pl.BlockSpec((B_HALO, B_WIDTH), lambda i: (nxt(i), ca)),
                  pl.BlockSpec((B_HALO, B_WIDTH), lambda i: (prev(i), cg)),
                  pl.BlockSpec((rb, B_WIDTH), lambda i: (i, cg)),
                  pl.BlockSpec((B_HALO, B_WIDTH), lambda i: (nxt(i), cg)),
                  pl.BlockSpec((B_KSIZE, B_WIDTH), lambda i: (0, 0)),
                  vec(), vec(), vec()],
        out_specs=pl.BlockSpec((rb, B_WIDTH), lambda i: (i, 0)),
        scratch_shapes=[pltpu.VMEM((rb + 2 * B_HALO, B_WIDTH), F32), pltpu.VMEM((rb, B_WIDTH), F32)],
        compiler_params=_cparams(("arbitrary",)),
        name="conv_module",
    )(p_rest, p_rest, p_rest, p_rest, p_rest, p_rest, conv_w,
      conv_b.reshape(1, B_WIDTH), ln_g.reshape(1, B_WIDTH), ln_b.reshape(1, B_WIDTH))


def _rope(x, cos, sin):
    lane = lax.broadcasted_iota(jnp.int32, x.shape, 1)
    quarter = C_HDIM // 4
    swapped = jnp.where((lane & (2 * quarter - 1)) < quarter,
                        pltpu.roll(x, C_HDIM - quarter, 1), pltpu.roll(x, quarter, 1))
    return x * cos + swapped * sin


def _attn_kernel(sink_ref, q_ref, kp_ref, kc_ref, kn_ref, vp_ref, vc_ref, vn_ref, ck_ref, cv_ref,
                 cq_ref, sq_ref, cp_ref, sp_ref, cc_ref, sc_ref, cn_ref, sn_ref, o_ref,
                 *, n_ctx_blocks, seq, n_ctx):
    i = pl.program_id(0)
    blk = C_BLOCK
    nloc = 3 * blk
    nq = C_GROUP * blk
    scale = C_HDIM ** -0.5

    row = lax.broadcasted_iota(jnp.int32, (nq, nloc), 0)
    col = lax.broadcasted_iota(jnp.int32, (nq, nloc), 1)
    t = row & (blk - 1)
    kpos = (i - n_ctx_blocks - 1) * blk + col
    valid = (col >= t) & (col <= t + 2 * blk) & (kpos >= 0) & (kpos < seq) & (i >= n_ctx_blocks)
    rgrp = lax.broadcasted_iota(jnp.int32, (nq, 1), 0) // blk

    cq, sq = cq_ref[...], sq_ref[...]
    for g in range(C_KVHEADS):
        ks = slice(g * C_HDIM, (g + 1) * C_HDIM)
        kparts = [_rope(kp_ref[:, ks].astype(F32), cp_ref[...], sp_ref[...]).astype(BF),
                  _rope(kc_ref[:, ks].astype(F32), cc_ref[...], sc_ref[...]).astype(BF),
                  _rope(kn_ref[:, ks].astype(F32), cn_ref[...], sn_ref[...]).astype(BF),
                  ck_ref[:, ks]]
        k_all = jnp.concatenate(kparts, axis=0)
        v_all = jnp.concatenate([vp_ref[:, ks], vc_ref[:, ks], vn_ref[:, ks], cv_ref[:, ks]], axis=0)
        qparts, sink_col = [], jnp.zeros((nq, 1), F32)
        for j in range(C_GROUP):
            h = g * C_GROUP + j
            qparts.append(_rope(q_ref[:, h * C_HDIM:(h + 1) * C_HDIM].astype(F32), cq, sq).astype(BF))
            sink_col = jnp.where(rgrp == j, sink_ref[h], sink_col)
        q_all = jnp.concatenate(qparts, axis=0)
        s = lax.dot_general(q_all, k_all, (((1,), (1,)), ((), ())), preferred_element_type=F32) * scale
        s_loc = jnp.where(valid, s[:, :nloc], -1e30)
        s_ctx = s[:, nloc:]
        m = jnp.maximum(jnp.maximum(jnp.max(s_loc, axis=-1, keepdims=True),
                                    jnp.max(s_ctx, axis=-1, keepdims=True)), sink_col)
        p_loc = jnp.exp(s_loc - m)
        p_ctx = jnp.exp(s_ctx - m)
        den = (jnp.sum(p_loc, axis=-1, keepdims=True) + jnp.sum(p_ctx, axis=-1, keepdims=True)
               + jnp.exp(sink_col - m))
        o = jnp.dot(p_loc.astype(BF), v_all[:nloc], preferred_element_type=F32)
        o = o + jnp.dot(p_ctx.astype(BF), v_all[nloc:], preferred_element_type=F32)
        o = o / den
        for j in range(C_GROUP):
            h = g * C_GROUP + j
            o_ref[:, h * C_HDIM:(h + 1) * C_HDIM] = o[j * blk:(j + 1) * blk].astype(o_ref.dtype)


def _window_attention(p_rest, rope_cos, rope_sin, sink, n_ctx, seq):
    r = p_rest.shape[0]
    blk = C_BLOCK
    nblk = r // blk
    ncb = n_ctx // blk
    qw = C_QHEADS * C_HDIM
    kvw = C_KVHEADS * C_HDIM
    qc, kc, vc = REST_CQ // qw, REST_CK // kvw, REST_CV // kvw

    def prev(i):
        return jnp.maximum(i - 1, 0)

    def nxt(i):
        return jnp.minimum(i + 1, nblk - 1)

    def kv(col, f):
        return pl.BlockSpec((blk, kvw), lambda i: (f(i), col))

    def tab(f):
        return pl.BlockSpec((blk, C_HDIM), lambda i: (f(i), 0))

    same = lambda i: i
    return pl.pallas_call(
        functools.partial(_attn_kernel, n_ctx_blocks=ncb, seq=seq, n_ctx=n_ctx),
        out_shape=jax.ShapeDtypeStruct((r, qw), BF),
        grid=(nblk,),
        in_specs=[pl.BlockSpec(memory_space=pltpu.SMEM),
                  pl.BlockSpec((blk, qw), lambda i: (i, qc)),
                  kv(kc, prev), kv(kc, same), kv(kc, nxt),
                  kv(vc, prev), kv(vc, same), kv(vc, nxt),
                  pl.BlockSpec((n_ctx, kvw), lambda i: (0, kc)),
                  pl.BlockSpec((n_ctx, kvw), lambda i: (0, vc)),
                  tab(same), tab(same), tab(prev), tab(prev), tab(same), tab(same), tab(nxt), tab(nxt)],
        out_specs=pl.BlockSpec((blk, qw), lambda i: (i, 0)),
        compiler_params=_cparams(("arbitrary",)),
        name="window_attn",
    )(sink, p_rest, p_rest, p_rest, p_rest, p_rest, p_rest, p_rest, p_rest, p_rest,
      rope_cos, rope_sin, rope_cos, rope_sin, rope_cos, rope_sin, rope_cos, rope_sin)


def _rope_tables(seq, n_ctx):
    pos = jnp.arange(seq)
    rows = (pos // GRID_W).astype(F32)
    cols = (pos % GRID_W).astype(F32)
    half = C_HDIM // 2
    inv = ROPE_BASE ** (-jnp.arange(0, half, 2, dtype=F32) / half)
    ar, ac = rows[:, None] * inv, cols[:, None] * inv
    cos = jnp.concatenate([jnp.cos(ar), jnp.cos(ar), jnp.cos(ac), jnp.cos(ac)], axis=-1)
    sin = jnp.concatenate([-jnp.sin(ar), jnp.sin(ar), -jnp.sin(ac), jnp.sin(ac)], axis=-1)
    cos = jnp.concatenate([jnp.ones((n_ctx, C_HDIM), F32), cos], axis=0)
    sin = jnp.concatenate([jnp.zeros((n_ctx, C_HDIM), F32), sin], axis=0)
    return cos, sin


def _merge_kernel(ya_ref, yb_ref, yc_ref, ga_ref, gb_ref, gc_ref, wa_ref, wb_ref, wc_ref, o_ref,
                  wab_ref, wbb_ref, wcb_ref):
    @pl.when(pl.program_id(1) == 0)
    def _():
        wab_ref[...] = wa_ref[0].astype(BF)
        wbb_ref[...] = wb_ref[0].astype(BF)
        wcb_ref[...] = wc_ref[0].astype(BF)

    def branch(y_ref, g_ref, w_ref):
        return jax.nn.sigmoid(g_ref[...].astype(F32)) * jnp.dot(y_ref[...], w_ref[...],
                                                                preferred_element_type=F32)
    acc = branch(ya_ref, ga_ref, wab_ref) + branch(yb_ref, gb_ref, wbb_ref) + branch(yc_ref, gc_ref, wcb_ref)
    o_ref[...] = acc.astype(o_ref.dtype)


def _merge(ya, yb, yc, p_rest, w_a, w_b, w_c, layer):
    r = ya.shape[0]
    tm, tn = _row_tile(r), 512
    gb0 = REST_GATES // tn
    nper = D // tn

    def y_spec():
        return pl.BlockSpec((tm, ya.shape[1]), lambda n, i: (i, 0))

    def g_spec(br):
        return pl.BlockSpec((tm, tn), lambda n, i: (i, gb0 + br * nper + n))

    def w_spec():
        return pl.BlockSpec((1, ya.shape[1], tn), lambda n, i: (layer, 0, n))

    return pl.pallas_call(
        _merge_kernel,
        out_shape=jax.ShapeDtypeStruct((r, D), BF),
        grid=(D // tn, r // tm),
        in_specs=[y_spec(), y_spec(), y_spec(), g_spec(0), g_spec(1), g_spec(2), w_spec(), w_spec(), w_spec()],
        out_specs=pl.BlockSpec((tm, tn), lambda n, i: (i, n)),
        scratch_shapes=[pltpu.VMEM((ya.shape[1], tn), BF) for _ in range(3)],
        compiler_params=_cparams(("arbitrary", "arbitrary")),
        name="branch_merge",
    )(ya, yb, yc, p_rest, p_rest, p_rest, w_a, w_b, w_c)


def _out_proj_kernel(m_ref, w_ref, x_ref, mod_ref, g_ref, xo_ref, h_ref, *, n_ctx_blocks):
    is_ctx = pl.program_id(0) < n_ctx_blocks
    y = jnp.dot(m_ref[...], w_ref[0], preferred_element_type=F32)
    x = x_ref[...] + _mod_row(mod_ref, is_ctx, 2) * y
    xo_ref[...] = x
    h = x * lax.rsqrt(jnp.mean(x * x, axis=-1, keepdims=True) + EPS) * g_ref[...]
    h_ref[...] = h * (1.0 + _mod_row(mod_ref, is_ctx, 4)) + _mod_row(mod_ref, is_ctx, 3)


def _out_proj(merged, w_out, x, mod, g2, layer, n_ctx_blocks):
    r = x.shape[0]
    rb = ROW_BLOCK
    return pl.pallas_call(
        functools.partial(_out_proj_kernel, n_ctx_blocks=n_ctx_blocks),
        out_shape=(jax.ShapeDtypeStruct((r, D), F32), jax.ShapeDtypeStruct((r, D), F32)),
        grid=(r // rb,),
        in_specs=[pl.BlockSpec((rb, D), lambda i: (i, 0)),
                  pl.BlockSpec((1, D, D), lambda i: (layer, 0, 0)),
                  pl.BlockSpec((rb, D), lambda i: (i, 0)),
                  pl.BlockSpec((8, 6 * D), lambda i: (0, 0)),
                  pl.BlockSpec((1, D), lambda i: (0, 0))],
        out_specs=(pl.BlockSpec((rb, D), lambda i: (i, 0)), pl.BlockSpec((rb, D), lambda i: (i, 0))),
        compiler_params=_cparams(("arbitrary",)),
        name="out_proj",
    )(merged, w_out, x, mod, g2.reshape(1, D))


def _router_kernel(h_ref, w_ref, o_ref):
    def split(x):
        hi = x.astype(BF)
        return hi, (x - hi.astype(F32)).astype(BF)

    def nt(a, b):
        return lax.dot_general(a, b, (((1,), (1,)), ((), ())), preferred_element_type=F32)

    w_hi, w_lo = split(w_ref[...])
    h_hi, h_lo = split(h_ref[...])
    logits = nt(w_hi, h_hi) + nt(w_lo, h_hi) + nt(w_hi, h_lo)
    e = jnp.exp(logits - jnp.max(logits, axis=0, keepdims=True))
    for j in range(o_ref.shape[0]):
        aff = e[:, j * LANES:(j + 1) * LANES]
        o_ref[j] = aff / jnp.sum(aff, axis=0, keepdims=True)


def _router(h2, w_router_t):
    r = h2.shape[0]
    nblk = r // LANES
    per = ROW_BLOCK // LANES
    return pl.pallas_call(
        _router_kernel,
        out_shape=jax.ShapeDtypeStruct((nblk, N_EXPERTS, LANES), F32),
        grid=(nblk // per,),
        in_specs=[pl.BlockSpec((ROW_BLOCK, D), lambda i: (i, 0)),
                  pl.BlockSpec((N_EXPERTS, D), lambda i: (0, 0))],
        out_specs=pl.BlockSpec((per, N_EXPERTS, LANES), lambda i: (i, 0, 0)),
        compiler_params=_cparams(("arbitrary",)),
        name="router",
    )(h2, w_router_t)


def _cumsum_tokens(m):
    nb = m.shape[0]
    m2 = m.reshape(nb * N_EXPERTS, LANES).astype(BF)
    li = lax.broadcasted_iota(jnp.int32, (LANES, LANES), 0)
    lj = lax.broadcasted_iota(jnp.int32, (LANES, LANES), 1)
    tri = jnp.where(li <= lj, 1.0, 0.0).astype(BF)
    ones = jnp.ones((LANES, LANES), BF)
    loc = jnp.dot(m2, tri, preferred_element_type=F32).reshape(nb, N_EXPERTS, LANES)
    tot = jnp.dot(m2, ones, preferred_element_type=F32).reshape(nb, N_EXPERTS, LANES)
    offs, run = [], jnp.zeros((N_EXPERTS, LANES), F32)
    for b in range(nb):
        offs.append(run)
        run = run + tot[b]
    before = jnp.stack(offs, axis=0)
    return loc + before, before


def _select_kernel(aff_ref, slot_ref, offs_ref, *, sets):
    for b0, nb, cap in sets:
        aff = aff_ref[b0:b0 + nb]

        def count(mask):
            c = jnp.sum(jnp.where(mask, 1.0, 0.0), axis=0)
            return jnp.sum(c, axis=1, keepdims=True)

        def as_float(word):
            return pltpu.bitcast(word, F32)

        def search(it, word):
            cand = word | jnp.left_shift(jnp.int32(1), 30 - it)
            return jnp.where(count(aff >= as_float(cand)[None]) >= cap, cand, word)

        thr = as_float(lax.fori_loop(0, 31, search, jnp.zeros((N_EXPERTS, 1), jnp.int32)))[None]
        gt = aff > thr
        eq = aff == thr
        need = (cap - count(gt))[None]
        eq_rank, _ = _cumsum_tokens(jnp.where(eq, 1.0, 0.0))
        sel = gt | (eq & (eq_rank <= need))
        rank, before = _cumsum_tokens(jnp.where(sel, 1.0, 0.0))
        slot_ref[b0:b0 + nb] = jnp.where(sel, rank - 1.0, -1.0)
        offs_ref[b0:b0 + nb] = before


def _select(aff, sets):
    shape = jax.ShapeDtypeStruct(aff.shape, F32)
    return pl.pallas_call(
        functools.partial(_select_kernel, sets=sets),
        out_shape=(shape, shape),
        compiler_params=pltpu.CompilerParams(vmem_limit_bytes=VMEM_LIMIT),
        name="expert_select",
    )(aff)


def _compact_kernel(offs_ref, slot_ref, aff_ref, idx_ref, gate_ref, acc_i, acc_g, *, sets):
    e = pl.program_id(0)
    win = LANES + 8
    lane = lax.broadcasted_iota(jnp.int32, (1, LANES), 1)
    w_iota = lax.broadcasted_iota(jnp.int32, (win, LANES), 0)
    slot0 = 0
    for b0, nb, cap in sets:
        acc_i[...] = jnp.zeros_like(acc_i)
        acc_g[...] = jnp.zeros_like(acc_g)

        def per_block(bi, carry):
            b = b0 + bi
            start = pl.multiple_of((offs_ref[b * N_EXPERTS + e] // 8) * 8, 8)
            hit = (w_iota + start).astype(F32) == slot_ref[b, pl.ds(e, 1), :]
            tok = (b * LANES + lane).astype(F32)
            acc_i[pl.ds(start, win), :] += jnp.where(hit, tok, 0.0)
            acc_g[pl.ds(start, win), :] += jnp.where(hit, aff_ref[b, pl.ds(e, 1), :], 0.0)
            return carry

        lax.fori_loop(0, nb, per_block, 0)
        idx_ref[0, slot0:slot0 + cap, :] = jnp.sum(acc_i[0:cap], axis=1, keepdims=True).astype(jnp.int32)
        gate_ref[0, slot0:slot0 + cap, :] = jnp.sum(acc_g[0:cap], axis=1, keepdims=True)
        slot0 += cap


def _compact(offs, slot, aff, sets):
    slots = sum(cap for _, _, cap in sets)
    max_cap = max(cap for _, _, cap in sets)
    whole = pl.BlockSpec(aff.shape, lambda e, offs: (0, 0, 0))
    out = pl.BlockSpec((1, slots, 1), lambda e, offs: (e, 0, 0))
    grid_spec = pltpu.PrefetchScalarGridSpec(
        num_scalar_prefetch=1,
        grid=(N_EXPERTS,),
        in_specs=[whole, whole],
        out_specs=(out, out),
        scratch_shapes=[pltpu.VMEM((max_cap + LANES + 8, LANES), F32) for _ in range(2)])
    return pl.pallas_call(
        functools.partial(_compact_kernel, sets=sets),
        out_shape=(jax.ShapeDtypeStruct((N_EXPERTS, slots, 1), jnp.int32),
                   jax.ShapeDtypeStruct((N_EXPERTS, slots, 1), F32)),
        grid_spec=grid_spec,
        compiler_params=_cparams(("arbitrary",)),
        name="expert_compact",
    )(offs, slot, aff)


def _routing_sets(n_ctx, seq):
    nb_ctx, nb_lat = n_ctx // LANES, seq // LANES
    return ((0, nb_ctx, EC_CAPACITY * n_ctx // N_EXPERTS), (nb_ctx, nb_lat, EC_CAPACITY * seq // N_EXPERTS))


def _ffn_kernel(idx_ref, h_hbm, x_hbm, gate_ref, mod_ref, wg_ref, wu_ref, wd_ref, xo_hbm,
                xs, xsb, xr, acc, sem_h, sem_x, sem_s, *, slots, ctx_slots, nf, n_experts):
    del x_hbm
    assert nf >= 4
    e = pl.program_id(0)
    f = pl.program_id(1)

    def gather_h(expert, p):
        row = idx_ref[expert * slots + p]
        return pltpu.make_async_copy(h_hbm.at[pl.ds(row, 1)], xs.at[pl.ds(p, 1)], sem_h)

    def all_h():
        return pltpu.make_async_copy(h_hbm.at[pl.ds(0, slots)], xs, sem_h)

    def all_x():
        return pltpu.make_async_copy(xo_hbm.at[pl.ds(0, slots)], xr, sem_x)

    def all_scatter():
        return pltpu.make_async_copy(xr, xo_hbm.at[pl.ds(0, slots)], sem_s)

    def ffn_chunk(first):
        xb = xsb[...]
        hid = _silu(jnp.dot(xb, wg_ref[0].astype(BF), preferred_element_type=F32)) * jnp.dot(
            xb, wu_ref[0].astype(BF), preferred_element_type=F32)
        out = jnp.dot(hid.astype(BF), wd_ref[0].astype(BF), preferred_element_type=F32)
        if first:
            acc[...] = out
        else:
            acc[...] += out

    @pl.when(jnp.logical_and(f == 0, e == 0))
    def _():
        def issue(grp, carry):
            p0 = pl.multiple_of(grp * ISSUE_UNROLL, ISSUE_UNROLL)
            for u in range(ISSUE_UNROLL):
                gather_h(0, p0 + u).start()
            return carry

        lax.fori_loop(0, slots // ISSUE_UNROLL, issue, 0)

    @pl.when(f == 0)
    def _():
        all_h().wait()
        xsb[...] = xs[...].astype(BF)
        ffn_chunk(True)

    @pl.when(jnp.logical_and(f == 1, e > 0))
    def _():
        all_scatter().wait()

    @pl.when(f == 1)
    def _():
        base = e * slots
        for p in range(slots):
            row = idx_ref[base + p]
            pltpu.make_async_copy(xo_hbm.at[pl.ds(row, 1)], xr.at[pl.ds(p, 1)], sem_x).start()
        ffn_chunk(False)

    @pl.when(f == 2)
    def _():
        nxt = jnp.minimum(e + 1, n_experts - 1)
        for p in range(slots):
            gather_h(nxt, p).start()
        ffn_chunk(False)

    @pl.when(jnp.logical_and(f > 2, f < nf - 1))
    def _():
        ffn_chunk(False)

    @pl.when(f == nf - 1)
    def _():
        all_x().wait()
        ffn_chunk(False)
        is_ctx = lax.broadcasted_iota(jnp.int32, (slots, 1), 0) < ctx_slots
        m5 = jnp.where(is_ctx, mod_ref[1:2, 5 * D:6 * D], mod_ref[0:1, 5 * D:6 * D])
        xr[...] = xr[...] + m5 * (acc[...] * gate_ref[0])
        base = e * slots
        for p in range(slots):
            row = idx_ref[base + p]
            pltpu.make_async_copy(xr.at[pl.ds(p, 1)], xo_hbm.at[pl.ds(row, 1)], sem_s).start()

    @pl.when(jnp.logical_and(f == nf - 1, e == n_experts - 1))
    def _():
        all_scatter().wait()
        all_h().wait()


def _expert_ffn(idx, h2, x, gate, mod, w_gate, w_up, w_down, layer, ctx_slots):
    n_exp, slots = idx.shape
    dff = w_gate.shape[-1]
    nf = dff // FFN_FCHUNK
    fc = FFN_FCHUNK
    grid_spec = pltpu.PrefetchScalarGridSpec(
        num_scalar_prefetch=1,
        grid=(n_exp, nf),
        in_specs=[pl.BlockSpec(memory_space=pl.ANY),
                  pl.BlockSpec(memory_space=pl.ANY),
                  pl.BlockSpec((1, slots, 1), lambda e, f, idx: (e, 0, 0)),
                  pl.BlockSpec((8, 6 * D), lambda e, f, idx: (0, 0)),
                  pl.BlockSpec((1, D, fc), lambda e, f, idx: (layer * n_exp + e, 0, f)),
                  pl.BlockSpec((1, D, fc), lambda e, f, idx: (layer * n_exp + e, 0, f)),
                  pl.BlockSpec((1, fc, D), lambda e, f, idx: (layer * n_exp + e, f, 0))],
        out_specs=pl.BlockSpec(memory_space=pl.ANY),
        scratch_shapes=[pltpu.VMEM((slots, D), F32), pltpu.VMEM((slots, D), BF),
                        pltpu.VMEM((slots, D), F32), pltpu.VMEM((slots, D), F32),
                        pltpu.SemaphoreType.DMA, pltpu.SemaphoreType.DMA, pltpu.SemaphoreType.DMA])
    return pl.pallas_call(
        functools.partial(_ffn_kernel, slots=slots, ctx_slots=ctx_slots, nf=nf, n_experts=n_exp),
        out_shape=jax.ShapeDtypeStruct(x.shape, F32),
        grid_spec=grid_spec,
        input_output_aliases={2: 0},
        compiler_params=_cparams(("arbitrary", "arbitrary")),
        name="expert_ffn",
    )(idx.reshape(-1), h2, x, gate, mod, w_gate, w_up, w_down)


def _final_norm_kernel(x_ref, g_ref, o_ref):
    x = x_ref[...]
    o_ref[...] = x * lax.rsqrt(jnp.mean(x * x, axis=-1, keepdims=True) + EPS) * g_ref[...]


def _final_norm(x, g, n_ctx_blocks):
    r = x.shape[0]
    rb = ROW_BLOCK
    nlat = r // rb - n_ctx_blocks
    return pl.pallas_call(
        _final_norm_kernel,
        out_shape=jax.ShapeDtypeStruct((nlat * rb, D), F32),
        grid=(nlat,),
        in_specs=[pl.BlockSpec((rb, D), lambda i: (i + n_ctx_blocks, 0)),
                  pl.BlockSpec((1, D), lambda i: (0, 0))],
        out_specs=pl.BlockSpec((rb, D), lambda i: (i, 0)),
        compiler_params=_cparams(("arbitrary",)),
        name="final_norm",
    )(x, g.reshape(1, D))


def _lower_bounds(lb_logits):
    p = jax.nn.softmax(lb_logits.astype(F32), axis=0)
    cs = jnp.cumsum(p, axis=0)
    return cs - cs[0]


def kernel(x, c, ctx, c_ctx, w_ada, b_ada, norm1_g, norm2_g, w_in, hgrn_lb_logits, hgrn_onorm_g,
           conv_w, conv_b, conv_ln_g, conv_ln_b, attn_sink, w_branch_a, w_branch_b, w_branch_c,
           w_out, w_router, w_exp_gate, w_exp_up, w_exp_down, final_norm_g):
    bsz, seq, _ = x.shape
    n_ctx = ctx.shape[1]
    depth = w_in.shape[0]
    assert bsz == 1 and x.shape[2] == D and w_in.shape[2] == D_IN
    assert n_ctx % ROW_BLOCK == 0 and seq % ROW_BLOCK == 0 and seq % GRID_W == 0
    ncb = n_ctx // ROW_BLOCK
    sets = _routing_sets(n_ctx, seq)
    ctx_slots = sets[0][2]

    xs = jnp.concatenate([ctx[0], x[0]], axis=0)
    c8 = jnp.zeros((8, D), F32).at[0].set(c[0]).at[1].set(c_ctx)
    mod = _modulation(c8, w_ada, b_ada)
    lbs = _lower_bounds(hgrn_lb_logits)
    rope_cos, rope_sin = _rope_tables(seq, n_ctx)

    w_a_b, w_b_b, w_c_b = w_branch_a, w_branch_b, w_branch_c
    w_out_b = w_out.astype(BF)
    n_exp = w_exp_gate.shape[1]
    w_g_b = w_exp_gate.reshape(depth * n_exp, D, -1)
    w_u_b = w_exp_up.reshape(depth * n_exp, D, -1)
    w_d_b = w_exp_down.reshape(depth * n_exp, -1, D)

    for l in range(depth):
        hx = _norm_mod(xs, norm1_g[l], mod[l], 0, 1, ncb, BF)
        p_qv = _matmul_cols(hx, w_in, l, SEG_QV[0], SEG_QV[1] - SEG_QV[0], BF, tn=1024)
        p_z = _matmul_cols(hx, w_in, l, SEG_Z[0], SEG_Z[1] - SEG_Z[0], F32, tn=1024)
        p_rest = _matmul_cols(hx, w_in, l, SEG_REST0, REST_WIDTH, BF, tn=512)

        o_f = _hgrn_forward(p_qv, p_z, lbs[l, 0])
        ya = _hgrn_backward(p_qv, p_z, p_rest, o_f, lbs[l, 1], hgrn_onorm_g[l], ncb)
        yb = _conv_module(p_rest, conv_w[l], conv_b[l], conv_ln_g[l], conv_ln_b[l], ncb)
        yc = _window_attention(p_rest, rope_cos, rope_sin, attn_sink[l], n_ctx, seq)

        merged = _merge(ya, yb, yc, p_rest, w_a_b, w_b_b, w_c_b, l)
        xs, h2 = _out_proj(merged, w_out_b, xs, mod[l], norm2_g[l], l, ncb)

        aff = _router(h2, w_router[l].T)
        slot, before = _select(aff, sets)
        offs = before[:, :, 0].astype(jnp.int32).reshape(-1)
        idx, gate = _compact(offs, slot, aff, sets)
        xs = _expert_ffn(idx.reshape(n_exp, -1), h2, xs, gate, mod[l], w_g_b, w_u_b, w_d_b, l, ctx_slots)

    return _final_norm(xs, final_norm_g, ncb)[None]
```
